```python
import jax, jax.numpy as jnp
from jax import lax
import numpy as np

D_MODEL = 2048
BATCH = 4
SEQ = 8192
DEPTH = 1
DEC_BATCH = 32
DEC_SEQ = 16
PAST_LEN = 2048

CHUNK = 64
N_LEFT_CHUNKS = 8
KV_WINDOW = N_LEFT_CHUNKS * CHUNK
BAND = KV_WINDOW + CHUNK
D_ATTN = D_MODEL // 2
N_HEADS_A = 8
HEAD_DIM = D_ATTN // N_HEADS_A
MAX_REL = 256
D_SGU = D_MODEL // 2
N_GROUPS_B = 8
GROUP_DIM_B = D_SGU // N_GROUPS_B
SGU_CHUNK = 128
D_FF = 5632
CONV_W = 3
EPS = 1e-6
SPLITS = (D_ATTN, 2 * D_ATTN, 3 * D_ATTN, 3 * D_ATTN + D_SGU, 3 * D_ATTN + 2 * D_SGU,
          3 * D_ATTN + 2 * D_SGU + D_MODEL)
D_IN = 3 * D_ATTN + 2 * D_SGU + 2 * D_MODEL

kernel_name = "hybrid_chunk_attn_sgu_convffn_step"


def rms_norm(x, g):
    x32 = x.astype(jnp.float32)
    inv = lax.rsqrt(jnp.mean(x32 * x32, axis=-1, keepdims=True) + EPS)
    return (x32 * inv).astype(x.dtype) * g


def mixer_inputs(x, norm_g, w_in, sgu_norm_g):
    B, T, _ = x.shape
    xn = rms_norm(x, norm_g)
    h = xn @ w_in
    q, k, v, u, vb, ga, gb = jnp.split(h, SPLITS, axis=-1)
    heads = lambda t: t.reshape(B, T, N_HEADS_A, HEAD_DIM)
    u = jax.nn.gelu(u, approximate=False)
    vb = rms_norm(jax.nn.gelu(vb, approximate=False), sgu_norm_g)
    return heads(q), heads(k), heads(v), u, vb, ga, gb


def band_attention(q, k, v, bias, mask):
    s = jnp.einsum('bqhd,bkhd->bhqk', q, k).astype(jnp.float32) * (HEAD_DIM ** -0.5)
    s = jnp.where(mask, s + bias.astype(jnp.float32), -1e30)
    p = jax.nn.softmax(s, axis=-1).astype(v.dtype)
    return jnp.einsum('bhqk,bkhd->bqhd', p, v)


def rel_bias_lookup(rel_bias, dist):
    return rel_bias[:, jnp.clip(dist, -MAX_REL, MAX_REL) + MAX_REL]


def chunk_attention_prompt(q, k, v, rel_bias):
    B, T, H, Dh = q.shape
    nc = T // CHUNK
    pad = ((0, 0), (KV_WINDOW, 0), (0, 0), (0, 0))
    kp, vp = jnp.pad(k, pad), jnp.pad(v, pad)
    i = jnp.arange(CHUNK)[:, None]
    j = jnp.arange(BAND)[None, :]
    bias = rel_bias_lookup(rel_bias, i - j + KV_WINDOW)
    qc = q.reshape(B, nc, CHUNK, H, Dh).transpose(1, 0, 2, 3, 4)

    def one_chunk(args):
        c, q_blk = args
        start = c * CHUNK
        k_band = lax.dynamic_slice_in_dim(kp, start, BAND, axis=1)
        v_band = lax.dynamic_slice_in_dim(vp, start, BAND, axis=1)
        k_pos = start - KV_WINDOW + jnp.arange(BAND)
        mask = jnp.broadcast_to((k_pos >= 0)[None, :], (CHUNK, BAND))
        return band_attention(q_blk, k_band, v_band, bias, mask)

    out = lax.map(one_chunk, (jnp.arange(nc), qc))
    return out.transpose(1, 0, 2, 3, 4).reshape(B, T, H * Dh)


def chunk_attention_sample(q, k_new, v_new, k_cache, v_cache, rel_bias):
    B, S, H, Dh = q.shape
    L = k_cache.shape[1]
    k = jnp.concatenate([k_cache, k_new], axis=1)
    v = jnp.concatenate([v_cache, v_new], axis=1)
    q_pos = PAST_LEN + jnp.arange(S)
    k_pos = jnp.concatenate([PAST_LEN - L + jnp.arange(L), PAST_LEN + jnp.arange(S)])
    cq = q_pos[:, None] // CHUNK
    ck = k_pos[None, :] // CHUNK
    mask = (ck <= cq) & (cq - ck <= N_LEFT_CHUNKS)
    bias = rel_bias_lookup(rel_bias, q_pos[:, None] - k_pos[None, :])
    return band_attention(q, k, v, bias, mask).reshape(B, S, H * Dh)


def causal_sgu_weights(w_s):
    return w_s * jnp.tril(jnp.ones((SGU_CHUNK, SGU_CHUNK), w_s.dtype))


def sgu_prompt(u, vb, w_s, b_s):
    B, T, _ = vb.shape
    nc = T // SGU_CHUNK
    vg = vb.reshape(B, nc, SGU_CHUNK, N_GROUPS_B, GROUP_DIM_B)
    mixed = jnp.einsum('gij,bcjgd->bcigd', causal_sgu_weights(w_s), vg)
    mixed = mixed + b_s.T[None, None, :, :, None]
    return u * mixed.reshape(B, T, D_SGU)


def sgu_sample(u, vb, w_s, b_s):
    B, S, _ = vb.shape
    vg = vb.reshape(B, S, N_GROUPS_B, GROUP_DIM_B)
    w = causal_sgu_weights(w_s)[:, :S, :S]
    mixed = jnp.einsum('gij,bjgd->bigd', w, vg) + b_s[:, :S].T[None, :, :, None]
    return u * mixed.reshape(B, S, D_SGU)


def merge_branches(x, a, s, ga, gb, w_branch_a, w_branch_b, w_out):
    m = jax.nn.sigmoid(ga) * (a @ w_branch_a) + jax.nn.sigmoid(gb) * (s @ w_branch_b)
    return x + m @ w_out


def conv_ffn(x, h_hist, norm_g, w_up, conv_w, conv_b, w_down):
    T = x.shape[1]
    h = rms_norm(x, norm_g) @ w_up
    h_ext = jnp.concatenate([h_hist, h], axis=1)
    hc = conv_b + sum(conv_w[t] * h_ext[:, t:t + T] for t in range(CONV_W))
    gate, val = jnp.split(hc, 2, axis=-1)
    y = x + (jax.nn.gelu(gate, approximate=False) * val) @ w_down
    return y, h_ext[:, -(CONV_W - 1):]


def layer_prompt(x, norm_mix_g, w_in, rel_bias, sgu_norm_g, w_s, b_s, w_branch_a, w_branch_b,
                 w_out, norm_ffn_g, w_up, conv_w, conv_b, w_down):
    B, T, _ = x.shape
    q, k, v, u, vb, ga, gb = mixer_inputs(x, norm_mix_g, w_in, sgu_norm_g)
    a = chunk_attention_prompt(q, k, v, rel_bias)
    s = sgu_prompt(u, vb, w_s, b_s)
    x = merge_branches(x, a, s, ga, gb, w_branch_a, w_branch_b, w_out)
    h_hist = jnp.zeros((B, CONV_W - 1, 2 * D_FF), x.dtype)
    x, conv_state = conv_ffn(x, h_hist, norm_ffn_g, w_up, conv_w, conv_b, w_down)
    keep = min(KV_WINDOW, T)
    return x, k[:, T - keep:], v[:, T - keep:], conv_state


def layer_sample(x, k_cache, v_cache, conv_cache, norm_mix_g, w_in, rel_bias, sgu_norm_g, w_s, b_s,
                 w_branch_a, w_branch_b, w_out, norm_ffn_g, w_up, conv_w, conv_b, w_down):
    q, k, v, u, vb, ga, gb = mixer_inputs(x, norm_mix_g, w_in, sgu_norm_g)
    a = chunk_attention_sample(q, k, v, k_cache, v_cache, rel_bias)
    s = sgu_sample(u, vb, w_s, b_s)
    x = merge_branches(x, a, s, ga, gb, w_branch_a, w_branch_b, w_out)
    x, conv_state = conv_ffn(x, conv_cache, norm_ffn_g, w_up, conv_w, conv_b, w_down)
    return x, k, v, vb, conv_state


def setup_inputs(seed: int = 0) -> dict:
    key = jax.random.key(seed)
    ks = jax.random.split(key, 24)
    f32 = jnp.float32
    nrm = lambda k, shape, scale: jax.random.normal(k, shape, f32) * scale
    L = min(KV_WINDOW, PAST_LEN)
    return {
        "x_prompt": nrm(ks[0], (BATCH, SEQ, D_MODEL), 1.0),
        "x_sample": nrm(ks[1], (DEC_BATCH, DEC_SEQ, D_MODEL), 1.0),
        "cache_k": nrm(ks[2], (DEPTH, DEC_BATCH, L, N_HEADS_A, HEAD_DIM), 1.0),
        "cache_v": nrm(ks[3], (DEPTH, DEC_BATCH, L, N_HEADS_A, HEAD_DIM), 1.0),
        "cache_ffn_conv": nrm(ks[4], (DEPTH, DEC_BATCH, CONV_W - 1, 2 * D_FF), 1.0),
        "norm_mix_g": 1.0 + nrm(ks[5], (DEPTH, D_MODEL), 0.02),
        "w_in": nrm(ks[6], (DEPTH, D_MODEL, D_IN), D_MODEL ** -0.5),
        "rel_bias": nrm(ks[7], (DEPTH, N_HEADS_A, 2 * MAX_REL + 1), 0.5),
        "sgu_norm_g": 1.0 + nrm(ks[8], (DEPTH, D_SGU), 0.02),
        "w_s": nrm(ks[9], (DEPTH, N_GROUPS_B, SGU_CHUNK, SGU_CHUNK), SGU_CHUNK ** -0.5),
        "b_s": 1.0 + nrm(ks[10], (DEPTH, N_GROUPS_B, SGU_CHUNK), 0.1),
        "w_branch_a": nrm(ks[11], (DEPTH, D_ATTN, D_MODEL), D_ATTN ** -0.5),
        "w_branch_b": nrm(ks[12], (DEPTH, D_SGU, D_MODEL), D_SGU ** -0.5),
        "w_out": nrm(ks[13], (DEPTH, D_MODEL, D_MODEL), D_MODEL ** -0.5),
        "norm_ffn_g": 1.0 + nrm(ks[14], (DEPTH, D_MODEL), 0.02),
        "w_up": nrm(ks[15], (DEPTH, D_MODEL, 2 * D_FF), D_MODEL ** -0.5),
        "conv_w": nrm(ks[16], (DEPTH, CONV_W, 2 * D_FF), CONV_W ** -0.5),
        "conv_b": nrm(ks[17], (DEPTH, 2 * D_FF), 0.01),
        "w_down": nrm(ks[18], (DEPTH, D_FF, D_MODEL), D_FF ** -0.5),
        "norm_final_g": 1.0 + nrm(ks[19], (D_MODEL,), 0.02),
    }


def reference(x_prompt, x_sample, cache_k, cache_v, cache_ffn_conv, norm_mix_g, w_in, rel_bias,
              sgu_norm_g, w_s, b_s, w_branch_a, w_branch_b, w_out, norm_ffn_g, w_up, conv_w, conv_b,
              w_down, norm_final_g):
    yp, ys = x_prompt, x_sample
    nk_p, nv_p, nc_p, nk_s, nv_s, nvb_s, nc_s = [], [], [], [], [], [], []
    for l in range(DEPTH):
        lp = (norm_mix_g[l], w_in[l], rel_bias[l], sgu_norm_g[l], w_s[l], b_s[l], w_branch_a[l],
              w_branch_b[l], w_out[l], norm_ffn_g[l], w_up[l], conv_w[l], conv_b[l], w_down[l])
        yp, kp_, vp_, cp_ = layer_prompt(yp, *lp)
        ys, ks_, vs_, vbs_, cs_ = layer_sample(ys, cache_k[l], cache_v[l], cache_ffn_conv[l], *lp)
        nk_p.append(kp_); nv_p.append(vp_); nc_p.append(cp_)
        nk_s.append(ks_); nv_s.append(vs_); nvb_s.append(vbs_); nc_s.append(cs_)
    y_prompt = rms_norm(yp, norm_final_g)
    y_sample = rms_norm(ys, norm_final_g)
    new_k_prompt = jnp.stack(nk_p)
    new_v_prompt = jnp.stack(nv_p)
    new_k_sample = jnp.stack(nk_s)
    new_v_sample = jnp.stack(nv_s)
    new_sgu_v_sample = jnp.stack(nvb_s)
    new_conv_prompt = jnp.stack(nc_p)
    new_conv_sample = jnp.stack(nc_s)
    return (y_prompt, y_sample, new_k_prompt, new_v_prompt, new_k_sample, new_v_sample,
            new_sgu_v_sample, new_conv_prompt, new_conv_sample)
```

```python
import functools

import numpy as np
import jax
import jax.numpy as jnp
from jax import lax
from jax.experimental import pallas as pl
from jax.experimental.pallas import tpu as pltpu

D_MODEL = 2048
CHUNK = 64
N_LEFT_CHUNKS = 8
KV_WINDOW = N_LEFT_CHUNKS * CHUNK
D_ATTN = D_MODEL // 2
N_HEADS = 8
HEAD_DIM = D_ATTN // N_HEADS
MAX_REL = 256
D_SGU = D_MODEL // 2
N_GROUPS = 8
GROUP_DIM = D_SGU // N_GROUPS
SGU_CHUNK = 128
D_FF = 5632
CONV_W = 3
EPS = 1e-6
PAST_LEN = 2048
D_IN = 3 * D_ATTN + 2 * D_SGU + 2 * D_MODEL
NEG_INF = -1e30

COL = 1024
COL_Q, COL_K, COL_V, COL_U, COL_VB, COL_GA, COL_GB = 0, 1, 2, 3, 4, 5, 7
N_COL_BLOCKS = D_IN // COL

ATTN_QB = 256
SGU_SUB = 256
FF_TILE = 512
N_FF_TILES = D_FF // FF_TILE
CARRY_ROWS = 8

VMEM_LIMIT = 56 * 1024 * 1024

BF16 = jnp.bfloat16
F32 = jnp.float32


def _rms(x, g):
    inv = lax.rsqrt(jnp.mean(x * x, axis=-1, keepdims=True) + EPS)
    return (x * inv) * g


def _gelu(x):
    return 0.5 * x * (1.0 + lax.erf(x * (2.0 ** -0.5)))


def _sigmoid(x):
    return 1.0 / (1.0 + jnp.exp(-x))


def _dot(a, b):
    return jnp.dot(a, b, preferred_element_type=F32)


def _dot_nt(a, b):
    return lax.dot_general(a, b, (((1,), (1,)), ((), ())), preferred_element_type=F32)


def _in_proj_kernel(x_ref, g_ref, w_ref, sg_ref, h_ref, kt_ref, vt_ref, vbt_ref, xn_ref, *, blocks_per_seq, tail):
    i = pl.program_id(0)
    j = pl.program_id(1)
    tm = x_ref.shape[0]
    is_tail = (i % blocks_per_seq) == (blocks_per_seq - 1)

    @pl.when(j == 0)
    def _():
        xn_ref[...] = _rms(x_ref[...], g_ref[...]).astype(BF16)

    acc = _dot(xn_ref[...], w_ref[...])

    @pl.when(j == COL_Q)
    def _():
        h_ref[...] = acc.astype(BF16)

    @pl.when(j == COL_K)
    def _():
        h_ref[...] = acc.astype(BF16)

        @pl.when(is_tail)
        def _():
            kt_ref[...] = acc[tm - tail:, :]

    @pl.when(j == COL_V)
    def _():
        h_ref[...] = acc.astype(BF16)

        @pl.when(is_tail)
        def _():
            vt_ref[...] = acc[tm - tail:, :]

    @pl.when(j == COL_U)
    def _():
        h_ref[...] = _gelu(acc).astype(BF16)

    @pl.when(j == COL_VB)
    def _():
        vb = _rms(_gelu(acc), sg_ref[...])
        h_ref[...] = vb.astype(BF16)

        @pl.when(is_tail)
        def _():
            vbt_ref[...] = vb[tm - tail:, :]

    @pl.when(j >= COL_GA)
    def _():
        h_ref[...] = _sigmoid(acc).astype(BF16)


def _in_proj(x, g, w_bf, sg, *, tm, rows_per_seq, tail):
    m = x.shape[0]
    blocks_per_seq = rows_per_seq // tm
    n_seq = m // rows_per_seq
    tail_spec = pl.BlockSpec((tail, COL), lambda i, j: (i // blocks_per_seq, 0))
    tail_shape = jax.ShapeDtypeStruct((n_seq * tail, COL), F32)
    return pl.pallas_call(
        functools.partial(_in_proj_kernel, blocks_per_seq=blocks_per_seq, tail=tail),
        grid=(m // tm, N_COL_BLOCKS),
        in_specs=[
            pl.BlockSpec((tm, D_MODEL), lambda i, j: (i, 0)),
            pl.BlockSpec((1, D_MODEL), lambda i, j: (0, 0)),
            pl.BlockSpec((D_MODEL, COL), lambda i, j: (0, j)),
            pl.BlockSpec((1, D_SGU), lambda i, j: (0, 0)),
        ],
        out_specs=[pl.BlockSpec((tm, COL), lambda i, j: (i, j)), tail_spec, tail_spec, tail_spec],
        out_shape=[jax.ShapeDtypeStruct((m, D_IN), BF16), tail_shape, tail_shape, tail_shape],
        scratch_shapes=[pltpu.VMEM((tm, D_MODEL), BF16)],
        compiler_params=pltpu.CompilerParams(
            dimension_semantics=("arbitrary", "arbitrary"), vmem_limit_bytes=VMEM_LIMIT),
        name="in_proj",
    )(x, g, w_bf, sg)


def _attn_prompt_kernel(q_ref, k0_ref, k1_ref, k2_ref, v0_ref, v1_ref, v2_ref, bias_ref, o_ref):
    qb = pl.program_id(1)
    k_refs = (k0_ref, k1_ref, k2_ref)
    v_refs = (v0_ref, v1_ref, v2_ref)
    pen = [jnp.where(qb - 2 + r >= 0, 0.0, NEG_INF).astype(F32) for r in range(3)]
    scale = HEAD_DIM ** -0.5
    for h in range(N_HEADS):
        cols = slice(h * HEAD_DIM, (h + 1) * HEAD_DIM)
        q = q_ref[:, cols]
        s = [
            _dot_nt(q, k_refs[r][:, cols]) * scale + bias_ref[h, :, r * ATTN_QB:(r + 1) * ATTN_QB] + pen[r]
            for r in range(3)
        ]
        mx = jnp.maximum(jnp.maximum(s[0].max(-1, keepdims=True), s[1].max(-1, keepdims=True)),
                         s[2].max(-1, keepdims=True))
        e = [jnp.exp(sr - mx) for sr in s]
        den = e[0].sum(-1, keepdims=True) + e[1].sum(-1, keepdims=True) + e[2].sum(-1, keepdims=True)
        o = (_dot(e[0].astype(BF16), v_refs[0][:, cols]) + _dot(e[1].astype(BF16), v_refs[1][:, cols])
             + _dot(e[2].astype(BF16), v_refs[2][:, cols]))
        o_ref[:, cols] = (o / den).astype(BF16)


def _prompt_bias(rel_bias):
    qi = np.arange(ATTN_QB)[:, None]
    kp = np.arange(3 * ATTN_QB)[None, :] - 2 * ATTN_QB
    idx = np.clip(qi - kp, -MAX_REL, MAX_REL) + MAX_REL
    cq, ck = qi // CHUNK, np.floor_divide(kp, CHUNK)
    allowed = (ck <= cq) & (cq - ck <= N_LEFT_CHUNKS)
    return jnp.where(allowed[None], rel_bias[:, idx].astype(F32), NEG_INF)


def _attn_prompt(hact, bias, *, batch, seq):
    nqb = seq // ATTN_QB
    kv_spec = lambda col, r: pl.BlockSpec(
        (ATTN_QB, COL), lambda b, t: (b * nqb + jnp.maximum(t - 2 + r, 0), col))
    return pl.pallas_call(
        _attn_prompt_kernel,
        grid=(batch, nqb),
        in_specs=[pl.BlockSpec((ATTN_QB, COL), lambda b, t: (b * nqb + t, COL_Q))]
        + [kv_spec(COL_K, r) for r in range(3)] + [kv_spec(COL_V, r) for r in range(3)]
        + [pl.BlockSpec((N_HEADS, ATTN_QB, 3 * ATTN_QB), lambda b, t: (0, 0, 0))],
        out_specs=pl.BlockSpec((ATTN_QB, D_ATTN), lambda b, t: (b * nqb + t, 0)),
        out_shape=jax.ShapeDtypeStruct((batch * seq, D_ATTN), BF16),
        compiler_params=pltpu.CompilerParams(
            dimension_semantics=("arbitrary", "arbitrary"), vmem_limit_bytes=VMEM_LIMIT),
        name="attn_prompt",
    )(hact, hact, hact, hact, hact, hact, hact, bias)


def _attn_sample_kernel(q_ref, kn_ref, vn_ref, kc_ref, vc_ref, bias_ref, o_ref):
    n_cache = kc_ref.shape[1]
    scale = HEAD_DIM ** -0.5
    for h in range(N_HEADS):
        cols = slice(h * HEAD_DIM, (h + 1) * HEAD_DIM)
        q = q_ref[:, cols]
        s_c = _dot_nt(q, kc_ref[0, :, cols].astype(BF16)) * scale + bias_ref[h, :, :n_cache]
        s_n = _dot_nt(q, kn_ref[:, cols]) * scale + bias_ref[h, :, n_cache:]
        mx = jnp.maximum(s_c.max(-1, keepdims=True), s_n.max(-1, keepdims=True))
        e_c, e_n = jnp.exp(s_c - mx), jnp.exp(s_n - mx)
        den = e_c.sum(-1, keepdims=True) + e_n.sum(-1, keepdims=True)
        o = _dot(e_c.astype(BF16), vc_ref[0, :, cols].astype(BF16)) + _dot(e_n.astype(BF16), vn_ref[:, cols])
        o_ref[:, cols] = (o / den).astype(BF16)


def _sample_bias(rel_bias, n_cache, n_new):
    q_pos = PAST_LEN + np.arange(n_new)
    k_pos = np.concatenate([PAST_LEN - n_cache + np.arange(n_cache), PAST_LEN + np.arange(n_new)])
    cq, ck = q_pos[:, None] // CHUNK, k_pos[None, :] // CHUNK
    allowed = (ck <= cq) & (cq - ck <= N_LEFT_CHUNKS)
    idx = np.clip(q_pos[:, None] - k_pos[None, :], -MAX_REL, MAX_REL) + MAX_REL
    return jnp.where(allowed[None], rel_bias[:, idx].astype(F32), NEG_INF)


def _attn_sample(hact, cache_k, cache_v, bias, *, n_streams, n_new):
    n_cache = cache_k.shape[1]
    new_spec = lambda col: pl.BlockSpec((n_new, COL), lambda b: (b, col))
    cache_spec = pl.BlockSpec((1, n_cache, D_ATTN), lambda b: (b, 0, 0))
    return pl.pallas_call(
        _attn_sample_kernel,
        grid=(n_streams,),
        in_specs=[new_spec(COL_Q), new_spec(COL_K), new_spec(COL_V), cache_spec, cache_spec,
                  pl.BlockSpec((N_HEADS, n_new, n_cache + n_new), lambda b: (0, 0, 0))],
        out_specs=pl.BlockSpec((n_new, D_ATTN), lambda b: (b, 0)),
        out_shape=jax.ShapeDtypeStruct((n_streams * n_new, D_ATTN), BF16),
        compiler_params=pltpu.CompilerParams(
            dimension_semantics=("arbitrary",), vmem_limit_bytes=VMEM_LIMIT),
        name="attn_sample",
    )(hact, hact, hact, cache_k, cache_v, bias)


def _merge_kernel(x_ref, u_ref, vb_ref, ga0_ref, ga1_ref, gb0_ref, gb1_ref, a_ref,
                  wbd_ref, sb_ref, wa_ref, wb_ref, wo_ref, o_ref, s_ref):
    tm = x_ref.shape[0]
    for c in range(tm // SGU_SUB):
        rows = slice(c * SGU_SUB, (c + 1) * SGU_SUB)
        for g in range(N_GROUPS):
            cols = slice(g * GROUP_DIM, (g + 1) * GROUP_DIM)
            mixed = _dot(wbd_ref[g], vb_ref[rows, cols]) + sb_ref[:, cols]
            s_ref[rows, cols] = (u_ref[rows, cols].astype(F32) * mixed).astype(BF16)
    pa = _dot(a_ref[...], wa_ref[...])
    pb = _dot(s_ref[...], wb_ref[...])
    half = D_MODEL // 2
    m0 = ga0_ref[...].astype(F32) * pa[:, :half] + gb0_ref[...].astype(F32) * pb[:, :half]
    m1 = ga1_ref[...].astype(F32) * pa[:, half:] + gb1_ref[...].astype(F32) * pb[:, half:]
    m = jnp.concatenate([m0, m1], axis=-1).astype(BF16)
    o_ref[...] = x_ref[...] + _dot(m, wo_ref[...])


def _sgu_block_weights(w_s, b_s, chunk):
    reps = SGU_SUB // chunk
    w = (w_s * jnp.tril(jnp.ones((SGU_CHUNK, SGU_CHUNK), w_s.dtype)))[:, :chunk, :chunk]
    eye = jnp.eye(reps, dtype=w.dtype)
    wbd = (eye[None, :, None, :, None] * w[:, None, :, None, :]).reshape(N_GROUPS, SGU_SUB, SGU_SUB)
    bias = jnp.repeat(jnp.tile(b_s[:, :chunk].T, (reps, 1)), GROUP_DIM, axis=1)
    return wbd.astype(BF16), bias.astype(F32)


def _merge(x, hact, a, wbd, sbias, wa, wb, wo, *, tm):
    m = x.shape[0]
    hcol = lambda col: pl.BlockSpec((tm, COL), lambda i: (i, col))
    const = lambda shape: pl.BlockSpec(shape, lambda i: (0,) * len(shape), pipeline_mode=pl.Buffered(1))
    return pl.pallas_call(
        _merge_kernel,
        grid=(m // tm,),
        in_specs=[pl.BlockSpec((tm, D_MODEL), lambda i: (i, 0)),
                  hcol(COL_U), hcol(COL_VB), hcol(COL_GA), hcol(COL_GA + 1), hcol(COL_GB), hcol(COL_GB + 1),
                  pl.BlockSpec((tm, D_ATTN), lambda i: (i, 0)),
                  const((N_GROUPS, SGU_SUB, SGU_SUB)), const((SGU_SUB, D_SGU)),
                  const((D_ATTN, D_MODEL)), const((D_SGU, D_MODEL)), const((D_MODEL, D_MODEL))],
        out_specs=pl.BlockSpec((tm, D_MODEL), lambda i: (i, 0)),
        out_shape=jax.ShapeDtypeStruct((m, D_MODEL), F32),
        scratch_shapes=[pltpu.VMEM((tm, D_SGU), BF16)],
        compiler_params=pltpu.CompilerParams(
            dimension_semantics=("arbitrary",), vmem_limit_bytes=VMEM_LIMIT),
        name="merge",
    )(x, hact, hact, hact, hact, hact, hact, a, wbd, sbias, wa, wb, wo)


def _conv_gate(hg, hv, prev_g, prev_v, cwg_ref, cwv_ref, cbg_ref, cbv_ref):
    def conv(h, prev, cw_ref, cb_ref):
        return cb_ref[...] + cw_ref[0:1, :] * prev(2) + cw_ref[1:2, :] * prev(1) + cw_ref[2:3, :] * h
    return (_gelu(conv(hg, prev_g, cwg_ref, cbg_ref)) * conv(hv, prev_v, cwv_ref, cbv_ref)).astype(BF16)


def _ffn_prologue(j, x_ref, g_ref, y_ref, xn_ref):
    @pl.when(j == 0)
    def _():
        x = x_ref[...]
        xn_ref[...] = _rms(x, g_ref[...]).astype(BF16)
        y_ref[...] = x


def _ffn_epilogue(j, gf_ref, y_ref):
    @pl.when(j == N_FF_TILES - 1)
    def _():
        y_ref[...] = _rms(y_ref[...], gf_ref[...])


def _ffn_prompt_kernel(x_ref, g_ref, wg_ref, wv_ref, cwg_ref, cwv_ref, cbg_ref, cbv_ref, wd_ref, gf_ref,
                       y_ref, hlg_ref, hlv_ref, xn_ref, cg_ref, cv_ref, *, blocks_per_seq):
    i = pl.program_id(0)
    j = pl.program_id(1)
    tm = x_ref.shape[0]
    _ffn_prologue(j, x_ref, g_ref, y_ref, xn_ref)
    hg = _dot(xn_ref[...], wg_ref[...])
    hv = _dot(xn_ref[...], wv_ref[...])

    @pl.when(i % blocks_per_seq == 0)
    def _():
        cg_ref[j] = jnp.zeros(cg_ref.shape[1:], F32)
        cv_ref[j] = jnp.zeros(cv_ref.shape[1:], F32)

    def delayed(h, carry_ref):
        carry = carry_ref[j]

        def prev(k):
            head = jnp.concatenate([carry, h[:CARRY_ROWS]], axis=0)[CARRY_ROWS - k:2 * CARRY_ROWS - k]
            return jnp.concatenate([head, pltpu.roll(h, k, axis=0)[CARRY_ROWS:]], axis=0)
        return prev

    act = _conv_gate(hg, hv, delayed(hg, cg_ref), delayed(hv, cv_ref), cwg_ref, cwv_ref, cbg_ref, cbv_ref)
    y_ref[...] += _dot(act, wd_ref[...])
    last_g, last_v = hg[tm - CARRY_ROWS:], hv[tm - CARRY_ROWS:]
    cg_ref[j] = last_g
    cv_ref[j] = last_v
    hlg_ref[0] = last_g
    hlv_ref[0] = last_v
    _ffn_epilogue(j, gf_ref, y_ref)


def _ffn_sample_kernel(x_ref, g_ref, wg_ref, wv_ref, cwg_ref, cwv_ref, cbg_ref, cbv_ref, wd_ref, gf_ref,
                       p0g_ref, p1g_ref, p0v_ref, p1v_ref, y_ref, hg_ref, hv_ref, xn_ref, *, seq):
    j = pl.program_id(1)
    tm = x_ref.shape[0]
    _ffn_prologue(j, x_ref, g_ref, y_ref, xn_ref)
    hg = _dot(xn_ref[...], wg_ref[...])
    hv = _dot(xn_ref[...], wv_ref[...])
    hg_ref[...] = hg
    hv_ref[...] = hv
    pos = lax.broadcasted_iota(jnp.int32, (tm, FF_TILE), 0) % seq

    def expand(ref):
        n = ref.shape[0]
        return jnp.broadcast_to(ref[...][:, None, :], (n, seq, FF_TILE)).reshape(tm, FF_TILE)

    def delayed(h, p0_ref, p1_ref):
        def prev(k):
            rolled = pltpu.roll(h, k, axis=0)
            if k == 1:
                return jnp.where(pos == 0, expand(p1_ref), rolled)
            return jnp.where(pos == 0, expand(p0_ref), jnp.where(pos == 1, expand(p1_ref), rolled))
        return prev

    act = _conv_gate(hg, hv, delayed(hg, p0g_ref, p1g_ref), delayed(hv, p0v_ref, p1v_ref),
                     cwg_ref, cwv_ref, cbg_ref, cbv_ref)
    y_ref[...] += _dot(act, wd_ref[...])
    _ffn_epilogue(j, gf_ref, y_ref)


def _ffn_common_specs(tm):
    row = lambda i, j: (i, 0)
    fixed = lambda i, j: (0, 0)
    gate = lambda i, j: (0, j)
    val = lambda i, j: (0, N_FF_TILES + j)
    return [
        pl.BlockSpec((tm, D_MODEL), row),
        pl.BlockSpec((1, D_MODEL), fixed),
        pl.BlockSpec((D_MODEL, FF_TILE), gate),
        pl.BlockSpec((D_MODEL, FF_TILE), val),
        pl.BlockSpec((CONV_W, FF_TILE), gate),
        pl.BlockSpec((CONV_W, FF_TILE), val),
        pl.BlockSpec((1, FF_TILE), gate),
        pl.BlockSpec((1, FF_TILE), val),
        pl.BlockSpec((FF_TILE, D_MODEL), lambda i, j: (j, 0)),
        pl.BlockSpec((1, D_MODEL), fixed),
    ]


def _ffn_prompt(x, g, w_up, conv_w, conv_b, w_down, gf, *, tm, rows_per_seq):
    m = x.shape[0]
    blocks_per_seq = rows_per_seq // tm
    last_spec = pl.BlockSpec((1, CARRY_ROWS, FF_TILE), lambda i, j: (i, 0, j))
    last_shape = jax.ShapeDtypeStruct((m // tm, CARRY_ROWS, D_FF), F32)
    carry = pltpu.VMEM((N_FF_TILES, CARRY_ROWS, FF_TILE), F32)
    return pl.pallas_call(
        functools.partial(_ffn_prompt_kernel, blocks_per_seq=blocks_per_seq),
        grid=(m // tm, N_FF_TILES),
        in_specs=_ffn_common_specs(tm),
        out_specs=[pl.BlockSpec((tm, D_MODEL), lambda i, j: (i, 0)), last_spec, last_spec],
        out_shape=[jax.ShapeDtypeStruct((m, D_MODEL), F32), last_shape, last_shape],
        scratch_shapes=[pltpu.VMEM((tm, D_MODEL), BF16), carry, carry],
        compiler_params=pltpu.CompilerParams(
            dimension_semantics=("arbitrary", "arbitrary"), vmem_limit_bytes=VMEM_LIMIT),
        name="ffn_prompt",
    )(x, g, w_up, w_up, conv_w, conv_w, conv_b, conv_b, w_down, gf)


def _ffn_sample(x, g, w_up, conv_w, conv_b, w_down, gf, hist0, hist1, *, seq):
    m = x.shape[0]
    n_streams = m // seq
    hist_g = pl.BlockSpec((n_streams, FF_TILE), lambda i, j: (0, j))
    hist_v = pl.BlockSpec((n_streams, FF_TILE), lambda i, j: (0, N_FF_TILES + j))
    h_spec = pl.BlockSpec((m, FF_TILE), lambda i, j: (0, j))
    h_shape = jax.ShapeDtypeStruct((m, D_FF), F32)
    return pl.pallas_call(
        functools.partial(_ffn_sample_kernel, seq=seq),
        grid=(1, N_FF_TILES),
        in_specs=_ffn_common_specs(m) + [hist_g, hist_g, hist_v, hist_v],
        out_specs=[pl.BlockSpec((m, D_MODEL), lambda i, j: (0, 0)), h_spec, h_spec],
        out_shape=[jax.ShapeDtypeStruct((m, D_MODEL), F32), h_shape, h_shape],
        scratch_shapes=[pltpu.VMEM((m, D_MODEL), BF16)],
        compiler_params=pltpu.CompilerParams(
            dimension_semantics=("arbitrary", "arbitrary"), vmem_limit_bytes=VMEM_LIMIT),
        name="ffn_sample",
    )(x, g, w_up, w_up, conv_w, conv_w, conv_b, conv_b, w_down, gf, hist0, hist1, hist0, hist1)


def kernel(x_prompt, x_sample, cache_k, cache_v, cache_ffn_conv, norm_mix_g, w_in, rel_bias, sgu_norm_g, w_s, b_s,
           w_branch_a, w_branch_b, w_out, norm_ffn_g, w_up, conv_w, conv_b, w_down, norm_final_g):
    depth = w_in.shape[0]
    assert depth == 1, "single-layer trunk"
    batch, seq, _ = x_prompt.shape
    n_streams, n_new, _ = x_sample.shape
    n_cache = cache_k.shape[2]
    keep = min(KV_WINDOW, seq)

    row = lambda v: v.reshape(1, -1).astype(F32)
    w_in_bf = w_in[0].astype(BF16)
    wa_bf, wb_bf, wo_bf = w_branch_a[0].astype(BF16), w_branch_b[0].astype(BF16), w_out[0].astype(BF16)
    w_up_bf, w_down_bf = w_up[0].astype(BF16), w_down[0].astype(BF16)
    g_mix, g_sgu, g_ffn, g_fin = row(norm_mix_g[0]), row(sgu_norm_g[0]), row(norm_ffn_g[0]), row(norm_final_g)
    cw, cb = conv_w[0].astype(F32), row(conv_b[0])

    tm = 512
    xp = x_prompt.reshape(batch * seq, D_MODEL)
    hact_p, k_tail, v_tail, _ = _in_proj(xp, g_mix, w_in_bf, g_sgu, tm=tm, rows_per_seq=seq, tail=keep)
    a_p = _attn_prompt(hact_p, _prompt_bias(rel_bias[0]), batch=batch, seq=seq)
    wbd_p, sb_p = _sgu_block_weights(w_s[0], b_s[0], SGU_CHUNK)
    x1_p = _merge(xp, hact_p, a_p, wbd_p, sb_p, wa_bf, wb_bf, wo_bf, tm=tm)
    y_p, hl_g, hl_v = _ffn_prompt(x1_p, g_ffn, w_up_bf, cw, cb, w_down_bf, g_fin, tm=tm, rows_per_seq=seq)

    ms = n_streams * n_new
    xs = x_sample.reshape(ms, D_MODEL)
    hact_s, k_new, v_new, vb_new = _in_proj(xs, g_mix, w_in_bf, g_sgu, tm=ms, rows_per_seq=ms, tail=ms)
    a_s = _attn_sample(hact_s, cache_k[0].reshape(n_streams, n_cache, D_ATTN),
                       cache_v[0].reshape(n_streams, n_cache, D_ATTN),
                       _sample_bias(rel_bias[0], n_cache, n_new), n_streams=n_streams, n_new=n_new)
    wbd_s, sb_s = _sgu_block_weights(w_s[0], b_s[0], n_new)
    x1_s = _merge(xs, hact_s, a_s, wbd_s, sb_s, wa_bf, wb_bf, wo_bf, tm=ms)
    y_s, h_g, h_v = _ffn_sample(x1_s, g_ffn, w_up_bf, cw, cb, w_down_bf, g_fin,
                                cache_ffn_conv[0, :, 0, :], cache_ffn_conv[0, :, 1, :], seq=n_new)

    hist = CONV_W - 1
    heads = lambda t, b, s: t.reshape(1, b, s, N_HEADS, HEAD_DIM)
    seq_end = slice(seq // tm - 1, None, seq // tm)
    new_conv_prompt = jnp.concatenate(
        [hl_g[seq_end, CARRY_ROWS - hist:], hl_v[seq_end, CARRY_ROWS - hist:]], axis=-1)[None]
    h_s = jnp.concatenate([h_g, h_v], axis=-1).reshape(n_streams, n_new, 2 * D_FF)
    new_conv_sample = h_s[:, n_new - hist:][None]
    return (
        y_p.reshape(batch, seq, D_MODEL),
        y_s.reshape(n_streams, n_new, D_MODEL),
        heads(k_tail, batch, keep),
        heads(v_tail, batch, keep),
        heads(k_new, n_streams, n_new),
        heads(v_new, n_streams, n_new),
        vb_new.reshape(1, n_streams, n_new, D_SGU),
        new_conv_prompt,
        new_conv_sample,
    )
```

```python
import functools

import numpy as np
import jax
import jax.numpy as jnp
from jax import lax
from jax.experimental import pallas as pl
from jax.experimental.pallas import tpu as pltpu

D_MODEL = 2048
CHUNK = 64
N_LEFT_CHUNKS = 8
KV_WINDOW = N_LEFT_CHUNKS * CHUNK
D_ATTN = D_MODEL // 2
N_HEADS = 8
HEAD_DIM = D_ATTN // N_HEADS
MAX_REL = 256
D_SGU = D_MODEL // 2
N_GROUPS = 8
GROUP_DIM = D_SGU // N_GROUPS
SGU_CHUNK = 128
D_FF = 5632
CONV_W = 3
EPS = 1e-6
PAST_LEN = 2048
D_IN = 3 * D_ATTN + 2 * D_SGU + 2 * D_MODEL
NEG_INF = -1e30

COL = 1024
COL_Q, COL_K, COL_V, COL_U, COL_VB, COL_GA, COL_GB = 0, 1, 2, 3, 4, 5, 7
N_COL_BLOCKS = D_IN // COL

PROJ_ROWS = 1024
MERGE_ROWS = 512
FFN_ROWS = 1024
FFN_SUB = 512
PROJ_COL_CHUNK = 512
PROJ_ROW_CHUNK = 256
ATTN_QB = 256
SGU_SUB = 256
FF_TILE = 512
N_FF_TILES = D_FF // FF_TILE
CARRY_ROWS = 8

VMEM_LIMIT = 56 * 1024 * 1024

BF16 = jnp.bfloat16
F32 = jnp.float32


def _rms(x, g):
    inv = lax.rsqrt(jnp.mean(x * x, axis=-1, keepdims=True) + EPS)
    return (x * inv) * g


def _gelu(x):
    return 0.5 * x * (1.0 + lax.erf(x * (2.0 ** -0.5)))


def _sigmoid(x):
    return 1.0 / (1.0 + jnp.exp(-x))


def _dot(a, b):
    return jnp.dot(a, b, preferred_element_type=F32)


def _dot_nt(a, b):
    return lax.dot_general(a, b, (((1,), (1,)), ((), ())), preferred_element_type=F32)


def _in_proj_kernel(x_ref, g_ref, w_ref, sg_ref, h_ref, kt_ref, vt_ref, *rest, tail):
    vbt_ref, xn_ref = rest if len(rest) == 2 else (None, rest[0])
    j = pl.program_id(1)
    tm = x_ref.shape[0]
    head = tm - tail

    @pl.when(j == 0)
    def _():
        xn_ref[...] = _rms(x_ref[...], g_ref[...]).astype(BF16)

    def by_cols(act, tail_ref=None):
        for c in range(COL // PROJ_COL_CHUNK):
            cols = slice(c * PROJ_COL_CHUNK, (c + 1) * PROJ_COL_CHUNK)
            acc = _dot(xn_ref[...], w_ref[:, cols])
            h_ref[:, cols] = act(acc).astype(BF16)
            if tail_ref is not None:
                tail_ref[:, cols] = acc[head:, :]

    @pl.when(j == COL_Q)
    def _():
        by_cols(lambda a: a)

    @pl.when(j == COL_K)
    def _():
        by_cols(lambda a: a, kt_ref)

    @pl.when(j == COL_V)
    def _():
        by_cols(lambda a: a, vt_ref)

    @pl.when(j == COL_U)
    def _():
        by_cols(_gelu)

    @pl.when(j == COL_VB)
    def _():
        for r in range(tm // PROJ_ROW_CHUNK):
            lo = r * PROJ_ROW_CHUNK
            rows = slice(lo, lo + PROJ_ROW_CHUNK)
            vb = _rms(_gelu(_dot(xn_ref[rows, :], w_ref[...])), sg_ref[...])
            h_ref[rows, :] = vb.astype(BF16)
            if vbt_ref is not None and lo >= head:
                vbt_ref[lo - head:lo - head + PROJ_ROW_CHUNK, :] = vb

    @pl.when(j >= COL_GA)
    def _():
        by_cols(_sigmoid)


def _in_proj(x, g, w_bf, sg, *, tm, rows_per_seq, tail, vb_tail):
    m = x.shape[0]
    blocks_per_seq = rows_per_seq // tm
    n_seq = m // rows_per_seq
    assert tail <= tm and (tm - tail) % PROJ_ROW_CHUNK == 0
    n_tails = 3 if vb_tail else 2
    tail_spec = pl.BlockSpec((tail, COL), lambda i, j: (i // blocks_per_seq, 0))
    tail_shape = jax.ShapeDtypeStruct((n_seq * tail, COL), F32)
    return pl.pallas_call(
        functools.partial(_in_proj_kernel, tail=tail),
        grid=(m // tm, N_COL_BLOCKS),
        in_specs=[
            pl.BlockSpec((tm, D_MODEL), lambda i, j: (i, 0)),
            pl.BlockSpec((1, D_MODEL), lambda i, j: (0, 0)),
            pl.BlockSpec((D_MODEL, COL), lambda i, j: (0, j)),
            pl.BlockSpec((1, D_SGU), lambda i, j: (0, 0)),
        ],
        out_specs=[pl.BlockSpec((tm, COL), lambda i, j: (i, j))] + [tail_spec] * n_tails,
        out_shape=[jax.ShapeDtypeStruct((m, D_IN), BF16)] + [tail_shape] * n_tails,
        scratch_shapes=[pltpu.VMEM((tm, D_MODEL), BF16)],
        compiler_params=pltpu.CompilerParams(
            dimension_semantics=("arbitrary", "arbitrary"), vmem_limit_bytes=VMEM_LIMIT),
        name="in_proj",
    )(x, g, w_bf, sg)


def _attn_prompt_kernel(q_ref, k0_ref, k1_ref, k2_ref, v0_ref, v1_ref, v2_ref, bias_ref, o_ref):
    qb = pl.program_id(1)
    k_refs = (k0_ref, k1_ref, k2_ref)
    v_refs = (v0_ref, v1_ref, v2_ref)
    pen = [jnp.where(qb - 2 + r >= 0, 0.0, NEG_INF).astype(F32) for r in range(3)]
    scale = HEAD_DIM ** -0.5
    for h in range(N_HEADS):
        cols = slice(h * HEAD_DIM, (h + 1) * HEAD_DIM)
        q = q_ref[:, cols]
        s = [
            _dot_nt(q, k_refs[r][:, cols]) * scale + bias_ref[h, :, r * ATTN_QB:(r + 1) * ATTN_QB] + pen[r]
            for r in range(3)
        ]
        mx = jnp.maximum(jnp.maximum(s[0].max(-1, keepdims=True), s[1].max(-1, keepdims=True)),
                         s[2].max(-1, keepdims=True))
        e = [jnp.exp(sr - mx) for sr in s]
        den = e[0].sum(-1, keepdims=True) + e[1].sum(-1, keepdims=True) + e[2].sum(-1, keepdims=True)
        o = (_dot(e[0].astype(BF16), v_refs[0][:, cols]) + _dot(e[1].astype(BF16), v_refs[1][:, cols])
             + _dot(e[2].astype(BF16), v_refs[2][:, cols]))
        o_ref[:, cols] = (o / den).astype(BF16)


def _rel_bias_table(rel_bias, n, m, d0):
    length = n + m - 1
    dist = np.arange(length) - (m - 1) + d0
    diag = rel_bias[:, np.clip(dist, -MAX_REL, MAX_REL) + MAX_REL].astype(F32)
    rev = diag[:, ::-1]
    padded = jnp.concatenate([rev, rev[:, :1]], axis=1)
    skew = jnp.tile(padded, (1, n))[:, :n * length].reshape(-1, n, length)
    return skew[:, :, n - 1:n - 1 + m]


def _prompt_bias(rel_bias):
    qi = np.arange(ATTN_QB)[:, None]
    kp = np.arange(3 * ATTN_QB)[None, :] - 2 * ATTN_QB
    cq, ck = qi // CHUNK, np.floor_divide(kp, CHUNK)
    allowed = (ck <= cq) & (cq - ck <= N_LEFT_CHUNKS)
    bias = _rel_bias_table(rel_bias, ATTN_QB, 3 * ATTN_QB, 2 * ATTN_QB)
    return jnp.where(allowed[None], bias, NEG_INF)


def _attn_prompt(hact, bias, *, batch, seq):
    nqb = seq // ATTN_QB
    kv_spec = lambda col, r: pl.BlockSpec(
        (ATTN_QB, COL), lambda b, t: (b * nqb + jnp.maximum(t - 2 + r, 0), col))
    return pl.pallas_call(
        _attn_prompt_kernel,
        grid=(batch, nqb),
        in_specs=[pl.BlockSpec((ATTN_QB, COL), lambda b, t: (b * nqb + t, COL_Q))]
        + [kv_spec(COL_K, r) for r in range(3)] + [kv_spec(COL_V, r) for r in range(3)]
        + [pl.BlockSpec((N_HEADS, ATTN_QB, 3 * ATTN_QB), lambda b, t: (0, 0, 0))],
        out_specs=pl.BlockSpec((ATTN_QB, D_ATTN), lambda b, t: (b * nqb + t, 0)),
        out_shape=jax.ShapeDtypeStruct((batch * seq, D_ATTN), BF16),
        compiler_params=pltpu.CompilerParams(
            dimension_semantics=("arbitrary", "arbitrary"), vmem_limit_bytes=VMEM_LIMIT),
        name="attn_prompt",
    )(hact, hact, hact, hact, hact, hact, hact, bias)


def _attn_sample_kernel(q_ref, kn_ref, vn_ref, kc_ref, vc_ref, bias_ref, o_ref):
    n_cache = kc_ref.shape[1]
    scale = HEAD_DIM ** -0.5
    for h in range(N_HEADS):
        cols = slice(h * HEAD_DIM, (h + 1) * HEAD_DIM)
        q = q_ref[:, cols]
        s_c = _dot_nt(q, kc_ref[0, :, cols].astype(BF16)) * scale + bias_ref[h, :, :n_cache]
        s_n = _dot_nt(q, kn_ref[:, cols]) * scale + bias_ref[h, :, n_cache:]
        mx = jnp.maximum(s_c.max(-1, keepdims=True), s_n.max(-1, keepdims=True))
        e_c, e_n = jnp.exp(s_c - mx), jnp.exp(s_n - mx)
        den = e_c.sum(-1, keepdims=True) + e_n.sum(-1, keepdims=True)
        o = _dot(e_c.astype(BF16), vc_ref[0, :, cols].astype(BF16)) + _dot(e_n.astype(BF16), vn_ref[:, cols])
        o_ref[:, cols] = (o / den).astype(BF16)


def _sample_bias(rel_bias, n_cache, n_new):
    q_pos = PAST_LEN + np.arange(n_new)
    k_pos = np.concatenate([PAST_LEN - n_cache + np.arange(n_cache), PAST_LEN + np.arange(n_new)])
    cq, ck = q_pos[:, None] // CHUNK, k_pos[None, :] // CHUNK
    allowed = (ck <= cq) & (cq - ck <= N_LEFT_CHUNKS)
    bias = jnp.concatenate([_rel_bias_table(rel_bias, n_new, n_cache, n_cache),
                            _rel_bias_table(rel_bias, n_new, n_new, 0)], axis=2)
    return jnp.where(allowed[None], bias, NEG_INF)


def _attn_sample(hact, cache_k, cache_v, bias, *, n_streams, n_new):
    n_cache = cache_k.shape[1]
    new_spec = lambda col: pl.BlockSpec((n_new, COL), lambda b: (b, col))
    cache_spec = pl.BlockSpec((1, n_cache, D_ATTN), lambda b: (b, 0, 0))
    return pl.pallas_call(
        _attn_sample_kernel,
        grid=(n_streams,),
        in_specs=[new_spec(COL_Q), new_spec(COL_K), new_spec(COL_V), cache_spec, cache_spec,
                  pl.BlockSpec((N_HEADS, n_new, n_cache + n_new), lambda b: (0, 0, 0))],
        out_specs=pl.BlockSpec((n_new, D_ATTN), lambda b: (b, 0)),
        out_shape=jax.ShapeDtypeStruct((n_streams * n_new, D_ATTN), BF16),
        compiler_params=pltpu.CompilerParams(
            dimension_semantics=("arbitrary",), vmem_limit_bytes=VMEM_LIMIT),
        name="attn_sample",
    )(hact, hact, hact, cache_k, cache_v, bias)


def _merge_kernel(x_ref, u_ref, vb_ref, ga0_ref, ga1_ref, gb0_ref, gb1_ref, a_ref,
                  wbd_ref, sb_ref, wa_ref, wb_ref, wo_ref, o_ref, s_ref):
    tm = x_ref.shape[0]
    for c in range(tm // SGU_SUB):
        rows = slice(c * SGU_SUB, (c + 1) * SGU_SUB)
        for g in range(N_GROUPS):
            cols = slice(g * GROUP_DIM, (g + 1) * GROUP_DIM)
            mixed = _dot(wbd_ref[g], vb_ref[rows, cols]) + sb_ref[:, cols]
            s_ref[rows, cols] = (u_ref[rows, cols].astype(F32) * mixed).astype(BF16)
    pa = _dot(a_ref[...], wa_ref[...])
    pb = _dot(s_ref[...], wb_ref[...])
    half = D_MODEL // 2
    m0 = ga0_ref[...].astype(F32) * pa[:, :half] + gb0_ref[...].astype(F32) * pb[:, :half]
    m1 = ga1_ref[...].astype(F32) * pa[:, half:] + gb1_ref[...].astype(F32) * pb[:, half:]
    m = jnp.concatenate([m0, m1], axis=-1).astype(BF16)
    o_ref[...] = x_ref[...] + _dot(m, wo_ref[...])


def _sgu_block_weights(w_s, b_s, chunk):
    reps = SGU_SUB // chunk
    w = (w_s * jnp.tril(jnp.ones((SGU_CHUNK, SGU_CHUNK), w_s.dtype)))[:, :chunk, :chunk]
    eye = jnp.eye(reps, dtype=w.dtype)
    wbd = (eye[None, :, None, :, None] * w[:, None, :, None, :]).reshape(N_GROUPS, SGU_SUB, SGU_SUB)
    bias = jnp.repeat(jnp.tile(b_s[:, :chunk].T, (reps, 1)), GROUP_DIM, axis=1)
    return wbd.astype(BF16), bias.astype(F32)


def _merge(x, hact, a, wbd, sbias, wa, wb, wo, *, tm):
    m = x.shape[0]
    hcol = lambda col: pl.BlockSpec((tm, COL), lambda i: (i, col))
    const = lambda shape: pl.BlockSpec(shape, lambda i: (0,) * len(shape), pipeline_mode=pl.Buffered(1))
    return pl.pallas_call(
        _merge_kernel,
        grid=(m // tm,),
        in_specs=[pl.BlockSpec((tm, D_MODEL), lambda i: (i, 0)),
                  hcol(COL_U), hcol(COL_VB), hcol(COL_GA), hcol(COL_GA + 1), hcol(COL_GB), hcol(COL_GB + 1),
                  pl.BlockSpec((tm, D_ATTN), lambda i: (i, 0)),
                  const((N_GROUPS, SGU_SUB, SGU_SUB)), const((SGU_SUB, D_SGU)),
                  const((D_ATTN, D_MODEL)), const((D_SGU, D_MODEL)), const((D_MODEL, D_MODEL))],
        out_specs=pl.BlockSpec((tm, D_MODEL), lambda i: (i, 0)),
        out_shape=jax.ShapeDtypeStruct((m, D_MODEL), F32),
        scratch_shapes=[pltpu.VMEM((tm, D_SGU), BF16)],
        compiler_params=pltpu.CompilerParams(
            dimension_semantics=("arbitrary",), vmem_limit_bytes=VMEM_LIMIT),
        name="merge",
    )(x, hact, hact, hact, hact, hact, hact, a, wbd, sbias, wa, wb, wo)


def _conv_gate(hg, hv, prev_g, prev_v, cwg_ref, cwv_ref, cbg_ref, cbv_ref):
    def conv(h, prev, cw_ref, cb_ref):
        return cb_ref[...] + cw_ref[0:1, :] * prev(2) + cw_ref[1:2, :] * prev(1) + cw_ref[2:3, :] * h
    return (_gelu(conv(hg, prev_g, cwg_ref, cbg_ref)) * conv(hv, prev_v, cwv_ref, cbv_ref)).astype(BF16)


def _ffn_prologue(j, x_ref, g_ref, y_ref, xn_ref):
    @pl.when(j == 0)
    def _():
        x = x_ref[...]
        xn_ref[...] = _rms(x, g_ref[...]).astype(BF16)
        y_ref[...] = x


def _ffn_epilogue(j, gf_ref, y_ref):
    @pl.when(j == N_FF_TILES - 1)
    def _():
        y_ref[...] = _rms(y_ref[...], gf_ref[...])


def _ffn_prompt_kernel(x_ref, g_ref, wg_ref, wv_ref, cwg_ref, cwv_ref, cbg_ref, cbv_ref, wd_ref, gf_ref,
                       y_ref, hlg_ref, hlv_ref, xn_ref, cg_ref, cv_ref, *, blocks_per_seq):
    i = pl.program_id(0)
    j = pl.program_id(1)
    tm = x_ref.shape[0]
    _ffn_prologue(j, x_ref, g_ref, y_ref, xn_ref)

    @pl.when(i % blocks_per_seq == 0)
    def _():
        cg_ref[j] = jnp.zeros(cg_ref.shape[1:], F32)
        cv_ref[j] = jnp.zeros(cv_ref.shape[1:], F32)

    def delayed(h, carry):
        def prev(k):
            head = jnp.concatenate([carry, h[:CARRY_ROWS]], axis=0)[CARRY_ROWS - k:2 * CARRY_ROWS - k]
            return jnp.concatenate([head, pltpu.roll(h, k, axis=0)[CARRY_ROWS:]], axis=0)
        return prev

    last_g, last_v = cg_ref[j], cv_ref[j]
    for s in range(tm // FFN_SUB):
        rows = slice(s * FFN_SUB, (s + 1) * FFN_SUB)
        hg = _dot(xn_ref[rows, :], wg_ref[...])
        hv = _dot(xn_ref[rows, :], wv_ref[...])
        act = _conv_gate(hg, hv, delayed(hg, last_g), delayed(hv, last_v), cwg_ref, cwv_ref, cbg_ref, cbv_ref)
        y_ref[rows, :] += _dot(act, wd_ref[...])
        last_g, last_v = hg[FFN_SUB - CARRY_ROWS:], hv[FFN_SUB - CARRY_ROWS:]
    cg_ref[j] = last_g
    cv_ref[j] = last_v
    hlg_ref[0] = last_g
    hlv_ref[0] = last_v
    _ffn_epilogue(j, gf_ref, y_ref)


def _ffn_sample_kernel(x_ref, g_ref, wg_ref, wv_ref, cwg_ref, cwv_ref, cbg_ref, cbv_ref, wd_ref, gf_ref,
                       pg_ref, pv_ref, y_ref, hg_ref, hv_ref, xn_ref, *, seq):
    j = pl.program_id(1)
    tm = x_ref.shape[0]
    _ffn_prologue(j, x_ref, g_ref, y_ref, xn_ref)
    hg = _dot(xn_ref[...], wg_ref[...])
    hv = _dot(xn_ref[...], wv_ref[...])
    hg_ref[...] = hg
    hv_ref[...] = hv
    pos = lax.broadcasted_iota(jnp.int32, (tm, FF_TILE), 0) % seq

    def delayed(h, hist_ref):
        def expand(t):
            n = hist_ref.shape[0]
            return jnp.broadcast_to(hist_ref[:, t:t + 1, :], (n, seq, FF_TILE)).reshape(tm, FF_TILE)

        def prev(k):
            rolled = pltpu.roll(h, k, axis=0)
            if k == 1:
                return jnp.where(pos == 0, expand(1), rolled)
            return jnp.where(pos == 0, expand(0), jnp.where(pos == 1, expand(1), rolled))
        return prev

    act = _conv_gate(hg, hv, delayed(hg, pg_ref), delayed(hv, pv_ref), cwg_ref, cwv_ref, cbg_ref, cbv_ref)
    y_ref[...] += _dot(act, wd_ref[...])
    _ffn_epilogue(j, gf_ref, y_ref)


def _ffn_common_specs(tm):
    row = lambda i, j: (i, 0)
    fixed = lambda i, j: (0, 0)
    gate = lambda i, j: (0, j)
    val = lambda i, j: (0, N_FF_TILES + j)
    return [
        pl.BlockSpec((tm, D_MODEL), row),
        pl.BlockSpec((1, D_MODEL), fixed),
        pl.BlockSpec((D_MODEL, FF_TILE), gate),
        pl.BlockSpec((D_MODEL, FF_TILE), val),
        pl.BlockSpec((CONV_W, FF_TILE), gate),
        pl.BlockSpec((CONV_W, FF_TILE), val),
        pl.BlockSpec((1, FF_TILE), gate),
        pl.BlockSpec((1, FF_TILE), val),
        pl.BlockSpec((FF_TILE, D_MODEL), lambda i, j: (j, 0)),
        pl.BlockSpec((1, D_MODEL), fixed),
    ]


def _ffn_prompt(x, g, w_up, conv_w, conv_b, w_down, gf, *, tm, rows_per_seq):
    m = x.shape[0]
    blocks_per_seq = rows_per_seq // tm
    last_spec = pl.BlockSpec((1, CARRY_ROWS, FF_TILE), lambda i, j: (i, 0, j))
    last_shape = jax.ShapeDtypeStruct((m // tm, CARRY_ROWS, D_FF), F32)
    carry = pltpu.VMEM((N_FF_TILES, CARRY_ROWS, FF_TILE), F32)
    return pl.pallas_call(
        functools.partial(_ffn_prompt_kernel, blocks_per_seq=blocks_per_seq),
        grid=(m // tm, N_FF_TILES),
        in_specs=_ffn_common_specs(tm),
        out_specs=[pl.BlockSpec((tm, D_MODEL), lambda i, j: (i, 0)), last_spec, last_spec],
        out_shape=[jax.ShapeDtypeStruct((m, D_MODEL), F32), last_shape, last_shape],
        scratch_shapes=[pltpu.VMEM((tm, D_MODEL), BF16), carry, carry],
        compiler_params=pltpu.CompilerParams(
            dimension_semantics=("arbitrary", "arbitrary"), vmem_limit_bytes=VMEM_LIMIT),
        name="ffn_prompt",
    )(x, g, w_up, w_up, conv_w, conv_w, conv_b, conv_b, w_down, gf)


def _ffn_sample(x, g, w_up, conv_w, conv_b, w_down, gf, hist, *, seq):
    m = x.shape[0]
    n_streams = m // seq
    hist_g = pl.BlockSpec((n_streams, CONV_W - 1, FF_TILE), lambda i, j: (0, 0, j))
    hist_v = pl.BlockSpec((n_streams, CONV_W - 1, FF_TILE), lambda i, j: (0, 0, N_FF_TILES + j))
    h_spec = pl.BlockSpec((m, FF_TILE), lambda i, j: (0, j))
    h_shape = jax.ShapeDtypeStruct((m, D_FF), F32)
    return pl.pallas_call(
        functools.partial(_ffn_sample_kernel, seq=seq),
        grid=(1, N_FF_TILES),
        in_specs=_ffn_common_specs(m) + [hist_g, hist_v],
        out_specs=[pl.BlockSpec((m, D_MODEL), lambda i, j: (0, 0)), h_spec, h_spec],
        out_shape=[jax.ShapeDtypeStruct((m, D_MODEL), F32), h_shape, h_shape],
        scratch_shapes=[pltpu.VMEM((m, D_MODEL), BF16)],
        compiler_params=pltpu.CompilerParams(
            dimension_semantics=("arbitrary", "arbitrary"), vmem_limit_bytes=VMEM_LIMIT),
        name="ffn_sample",
    )(x, g, w_up, w_up, conv_w, conv_w, conv_b, conv_b, w_down, gf, hist, hist)


def kernel(x_prompt, x_sample, cache_k, cache_v, cache_ffn_conv, norm_mix_g, w_in, rel_bias, sgu_norm_g, w_s, b_s,
           w_branch_a, w_branch_b, w_out, norm_ffn_g, w_up, conv_w, conv_b, w_down, norm_final_g):
    depth = w_in.shape[0]
    assert depth == 1, "single-layer trunk"
    batch, seq, _ = x_prompt.shape
    n_streams, n_new, _ = x_sample.shape
    n_cache = cache_k.shape[2]
    keep = min(KV_WINDOW, seq)

    row = lambda v: v.reshape(1, -1).astype(F32)
    w_in_bf = w_in[0].astype(BF16)
    wa_bf, wb_bf, wo_bf = w_branch_a[0].astype(BF16), w_branch_b[0].astype(BF16), w_out[0].astype(BF16)
    w_up_bf, w_down_bf = w_up[0].astype(BF16), w_down[0].astype(BF16)
    g_mix, g_sgu, g_ffn, g_fin = row(norm_mix_g[0]), row(sgu_norm_g[0]), row(norm_ffn_g[0]), row(norm_final_g)
    cw, cb = conv_w[0].astype(F32), row(conv_b[0])

    xp = x_prompt.reshape(batch * seq, D_MODEL)
    hact_p, k_tail, v_tail = _in_proj(xp, g_mix, w_in_bf, g_sgu, tm=min(PROJ_ROWS, seq), rows_per_seq=seq,
                                      tail=keep, vb_tail=False)
    a_p = _attn_prompt(hact_p, _prompt_bias(rel_bias[0]), batch=batch, seq=seq)
    wbd_p, sb_p = _sgu_block_weights(w_s[0], b_s[0], SGU_CHUNK)
    x1_p = _merge(xp, hact_p, a_p, wbd_p, sb_p, wa_bf, wb_bf, wo_bf, tm=MERGE_ROWS)
    ffn_rows = min(FFN_ROWS, seq)
    y_p, hl_g, hl_v = _ffn_prompt(x1_p, g_ffn, w_up_bf, cw, cb, w_down_bf, g_fin, tm=ffn_rows, rows_per_seq=seq)

    ms = n_streams * n_new
    xs = x_sample.reshape(ms, D_MODEL)
    hact_s, k_new, v_new, vb_new = _in_proj(xs, g_mix, w_in_bf, g_sgu, tm=ms, rows_per_seq=ms, tail=ms,
                                            vb_tail=True)
    a_s = _attn_sample(hact_s, cache_k[0].reshape(n_streams, n_cache, D_ATTN),
                       cache_v[0].reshape(n_streams, n_cache, D_ATTN),
                       _sample_bias(rel_bias[0], n_cache, n_new), n_streams=n_streams, n_new=n_new)
    wbd_s, sb_s = _sgu_block_weights(w_s[0], b_s[0], n_new)
    x1_s = _merge(xs, hact_s, a_s, wbd_s, sb_s, wa_bf, wb_bf, wo_bf, tm=ms)
    y_s, h_g, h_v = _ffn_sample(x1_s, g_ffn, w_up_bf, cw, cb, w_down_bf, g_fin,
                                cache_ffn_conv[0], seq=n_new)

    hist = CONV_W - 1
    heads = lambda t, b, s: t.reshape(1, b, s, N_HEADS, HEAD_DIM)
    seq_end = slice(seq // ffn_rows - 1, None, seq // ffn_rows)
    new_conv_prompt = jnp.concatenate(
        [hl_g[seq_end, CARRY_ROWS - hist:], hl_v[seq_end, CARRY_ROWS - hist:]], axis=-1)[None]
    h_s = jnp.concatenate([h_g, h_v], axis=-1).reshape(n_streams, n_new, 2 * D_FF)
    new_conv_sample = h_s[:, n_new - hist:][None]
    return (
        y_p.reshape(batch, seq, D_MODEL),
        y_s.reshape(n_streams, n_new, D_MODEL),
        heads(k_tail, batch, keep),
        heads(v_tail, batch, keep),
        heads(k_new, n_streams, n_new),
        heads(v_new, n_streams, n_new),
        vb_new.reshape(1, n_streams, n_new, D_SGU),
        new_conv_prompt,
        new_conv_sample,
    )
```

```python
import functools

import numpy as np
import jax
import jax.numpy as jnp
from jax import lax
from jax.experimental import pallas as pl
from jax.experimental.pallas import tpu as pltpu

D_MODEL = 2048
CHUNK = 64
N_LEFT_CHUNKS = 8
KV_WINDOW = N_LEFT_CHUNKS * CHUNK
D_ATTN = D_MODEL // 2
N_HEADS = 8
HEAD_DIM = D_ATTN // N_HEADS
MAX_REL = 256
D_SGU = D_MODEL // 2
N_GROUPS = 8
GROUP_DIM = D_SGU // N_GROUPS
SGU_CHUNK = 128
D_FF = 5632
CONV_W = 3
EPS = 1e-6
PAST_LEN = 2048
D_IN = 3 * D_ATTN + 2 * D_SGU + 2 * D_MODEL
NEG_INF = -1e30

COL = 1024
COL_Q, COL_K, COL_V, COL_U, COL_VB, COL_GA, COL_GB = 0, 1, 2, 3, 4, 5, 7
N_COL_BLOCKS = D_IN // COL

PROJ_ROWS = 1024
MERGE_ROWS = 512
FFN_ROWS = 1024
FFN_SUB = 1024
PROJ_COL_CHUNK = 512
PROJ_ROW_CHUNK = 256
ATTN_QB = 256
ATTN_QSUB = 256
LOG2E = float(np.log2(np.e))
SGU_SUB = 256
FF_TILE = 512
N_FF_TILES = D_FF // FF_TILE
CARRY_ROWS = 8

VMEM_LIMIT = 56 * 1024 * 1024
VMEM_LIMIT_FFN = 60 * 1024 * 1024

BF16 = jnp.bfloat16
F32 = jnp.float32


def _rms(x, g):
    inv = lax.rsqrt(jnp.mean(x * x, axis=-1, keepdims=True) + EPS)
    return (x * inv) * g


def _gelu(x):
    return 0.5 * x * (1.0 + lax.erf(x * (2.0 ** -0.5)))


def _sigmoid(x):
    return 0.5 * jnp.tanh(0.5 * x) + 0.5


def _dot(a, b):
    return jnp.dot(a, b, preferred_element_type=F32)


def _dot_nt(a, b):
    return lax.dot_general(a, b, (((1,), (1,)), ((), ())), preferred_element_type=F32)


def _in_proj_kernel(x_ref, g_ref, w_ref, sg_ref, h_ref, kt_ref, vt_ref, *rest, tail):
    vbt_ref, xn_ref = rest if len(rest) == 2 else (None, rest[0])
    j = pl.program_id(1)
    tm = x_ref.shape[0]
    head = tm - tail

    @pl.when(j == 0)
    def _():
        xn_ref[...] = _rms(x_ref[...], g_ref[...]).astype(BF16)

    def by_cols(act, tail_ref=None):
        for c in range(COL // PROJ_COL_CHUNK):
            cols = slice(c * PROJ_COL_CHUNK, (c + 1) * PROJ_COL_CHUNK)
            acc = _dot(xn_ref[...], w_ref[:, cols])
            h_ref[:, cols] = act(acc).astype(BF16)
            if tail_ref is not None:
                tail_ref[:, cols] = acc[head:, :]

    @pl.when(j == COL_Q)
    def _():
        by_cols(lambda a: a)

    @pl.when(j == COL_K)
    def _():
        by_cols(lambda a: a, kt_ref)

    @pl.when(j == COL_V)
    def _():
        by_cols(lambda a: a, vt_ref)

    @pl.when(j == COL_U)
    def _():
        by_cols(_gelu)

    @pl.when(j == COL_VB)
    def _():
        for r in range(tm // PROJ_ROW_CHUNK):
            lo = r * PROJ_ROW_CHUNK
            rows = slice(lo, lo + PROJ_ROW_CHUNK)
            vb = _rms(_gelu(_dot(xn_ref[rows, :], w_ref[...])), sg_ref[...])
            h_ref[rows, :] = vb.astype(BF16)
            if vbt_ref is not None and lo >= head:
                vbt_ref[lo - head:lo - head + PROJ_ROW_CHUNK, :] = vb

    @pl.when(j >= COL_GA)
    def _():
        by_cols(_sigmoid)


def _in_proj(x, g, w_bf, sg, *, tm, rows_per_seq, tail, vb_tail):
    m = x.shape[0]
    blocks_per_seq = rows_per_seq // tm
    n_seq = m // rows_per_seq
    assert tail <= tm and (tm - tail) % PROJ_ROW_CHUNK == 0
    n_tails = 3 if vb_tail else 2
    tail_spec = pl.BlockSpec((tail, COL), lambda i, j: (i // blocks_per_seq, 0))
    tail_shape = jax.ShapeDtypeStruct((n_seq * tail, COL), F32)
    return pl.pallas_call(
        functools.partial(_in_proj_kernel, tail=tail),
        grid=(m // tm, N_COL_BLOCKS),
        in_specs=[
            pl.BlockSpec((tm, D_MODEL), lambda i, j: (i, 0)),
            pl.BlockSpec((1, D_MODEL), lambda i, j: (0, 0)),
            pl.BlockSpec((D_MODEL, COL), lambda i, j: (0, j)),
            pl.BlockSpec((1, D_SGU), lambda i, j: (0, 0)),
        ],
        out_specs=[pl.BlockSpec((tm, COL), lambda i, j: (i, j))] + [tail_spec] * n_tails,
        out_shape=[jax.ShapeDtypeStruct((m, D_IN), BF16)] + [tail_shape] * n_tails,
        scratch_shapes=[pltpu.VMEM((tm, D_MODEL), BF16)],
        compiler_params=pltpu.CompilerParams(
            dimension_semantics=("arbitrary", "arbitrary"), vmem_limit_bytes=VMEM_LIMIT),
        name="in_proj",
    )(x, g, w_bf, sg)


def _attn_prompt_kernel(q_ref, k0_ref, k1_ref, k2_ref, v0_ref, v1_ref, v2_ref, bias_ref, o_ref):
    qb = pl.program_id(1)
    k_refs = (k0_ref, k1_ref, k2_ref)
    v_refs = (v0_ref, v1_ref, v2_ref)
    scale2 = HEAD_DIM ** -0.5 * LOG2E

    def attend(mask_missing):
        for h in range(N_HEADS):
            cols = slice(h * HEAD_DIM, (h + 1) * HEAD_DIM)
            for lo in range(0, ATTN_QB, ATTN_QSUB):
                rows = slice(lo, lo + ATTN_QSUB)
                q = q_ref[rows, cols]
                s, v = [], []
                for r in range(3):
                    a, b = max(lo, r * ATTN_QB), min(lo + KV_WINDOW + ATTN_QSUB, (r + 1) * ATTN_QB)
                    if a >= b:
                        continue
                    keys = slice(a - r * ATTN_QB, b - r * ATTN_QB)
                    sr = _dot_nt(q, k_refs[r][keys, cols]) * scale2 + bias_ref[h, rows, a:b]
                    if mask_missing:
                        sr = sr + jnp.where(qb - 2 + r >= 0, 0.0, NEG_INF).astype(F32)
                    s.append(sr)
                    v.append(v_refs[r][keys, cols])
                mx = functools.reduce(jnp.maximum, [sr.max(-1, keepdims=True) for sr in s])
                e = [jnp.exp2(sr - mx) for sr in s]
                den = sum(er.sum(-1, keepdims=True) for er in e)
                o = sum(_dot(er.astype(BF16), vr) for er, vr in zip(e, v))
                o_ref[rows, cols] = (o / den).astype(BF16)

    @pl.when(qb < KV_WINDOW // ATTN_QB)
    def _():
        attend(True)

    @pl.when(qb >= KV_WINDOW // ATTN_QB)
    def _():
        attend(False)


def _rel_bias_table(rel_bias, n, m, d0):
    length = n + m - 1
    dist = np.arange(length) - (m - 1) + d0
    diag = rel_bias[:, np.clip(dist, -MAX_REL, MAX_REL) + MAX_REL].astype(F32)
    rev = diag[:, ::-1]
    padded = jnp.concatenate([rev, rev[:, :1]], axis=1)
    skew = jnp.tile(padded, (1, n))[:, :n * length].reshape(-1, n, length)
    return skew[:, :, n - 1:n - 1 + m]


def _prompt_bias(rel_bias):
    qi = np.arange(ATTN_QB)[:, None]
    kp = np.arange(3 * ATTN_QB)[None, :] - 2 * ATTN_QB
    cq, ck = qi // CHUNK, np.floor_divide(kp, CHUNK)
    allowed = (ck <= cq) & (cq - ck <= N_LEFT_CHUNKS)
    bias = _rel_bias_table(rel_bias, ATTN_QB, 3 * ATTN_QB, 2 * ATTN_QB) * LOG2E
    return jnp.where(allowed[None], bias, NEG_INF)


def _attn_prompt(hact, bias, *, batch, seq):
    nqb = seq // ATTN_QB
    kv_spec = lambda col, r: pl.BlockSpec(
        (ATTN_QB, COL), lambda b, t: (b * nqb + jnp.maximum(t - 2 + r, 0), col))
    return pl.pallas_call(
        _attn_prompt_kernel,
        grid=(batch, nqb),
        in_specs=[pl.BlockSpec((ATTN_QB, COL), lambda b, t: (b * nqb + t, COL_Q))]
        + [kv_spec(COL_K, r) for r in range(3)] + [kv_spec(COL_V, r) for r in range(3)]
        + [pl.BlockSpec((N_HEADS, ATTN_QB, 3 * ATTN_QB), lambda b, t: (0, 0, 0))],
        out_specs=pl.BlockSpec((ATTN_QB, D_ATTN), lambda b, t: (b * nqb + t, 0)),
        out_shape=jax.ShapeDtypeStruct((batch * seq, D_ATTN), BF16),
        compiler_params=pltpu.CompilerParams(
            dimension_semantics=("arbitrary", "arbitrary"), vmem_limit_bytes=VMEM_LIMIT),
        name="attn_prompt",
    )(hact, hact, hact, hact, hact, hact, hact, bias)


def _attn_sample_kernel(q_ref, kn_ref, vn_ref, kc_ref, vc_ref, bias_ref, o_ref):
    n_cache = kc_ref.shape[1] // N_HEADS
    scale = HEAD_DIM ** -0.5
    for h in range(N_HEADS):
        cols = slice(h * HEAD_DIM, (h + 1) * HEAD_DIM)
        head_rows = pl.ds(h, n_cache, stride=N_HEADS)
        q = q_ref[:, cols]
        s_c = _dot_nt(q, kc_ref[0, head_rows, :].astype(BF16)) * scale + bias_ref[h, :, :n_cache]
        s_n = _dot_nt(q, kn_ref[:, cols]) * scale + bias_ref[h, :, n_cache:]
        mx = jnp.maximum(s_c.max(-1, keepdims=True), s_n.max(-1, keepdims=True))
        e_c, e_n = jnp.exp(s_c - mx), jnp.exp(s_n - mx)
        den = e_c.sum(-1, keepdims=True) + e_n.sum(-1, keepdims=True)
        o = (_dot(e_c.astype(BF16), vc_ref[0, head_rows, :].astype(BF16))
             + _dot(e_n.astype(BF16), vn_ref[:, cols]))
        o_ref[:, cols] = (o / den).astype(BF16)


def _sample_bias(rel_bias, n_cache, n_new):
    q_pos = PAST_LEN + np.arange(n_new)
    k_pos = np.concatenate([PAST_LEN - n_cache + np.arange(n_cache), PAST_LEN + np.arange(n_new)])
    cq, ck = q_pos[:, None] // CHUNK, k_pos[None, :] // CHUNK
    allowed = (ck <= cq) & (cq - ck <= N_LEFT_CHUNKS)
    bias = jnp.concatenate([_rel_bias_table(rel_bias, n_new, n_cache, n_cache),
                            _rel_bias_table(rel_bias, n_new, n_new, 0)], axis=2)
    return jnp.where(allowed[None], bias, NEG_INF)


def _attn_sample(hact, cache_k, cache_v, bias, *, n_streams, n_new):
    n_cache = cache_k.shape[1] // N_HEADS
    new_spec = lambda col: pl.BlockSpec((n_new, COL), lambda b: (b, col))
    cache_spec = pl.BlockSpec((1, n_cache * N_HEADS, HEAD_DIM), lambda b: (b, 0, 0))
    return pl.pallas_call(
        _attn_sample_kernel,
        grid=(n_streams,),
        in_specs=[new_spec(COL_Q), new_spec(COL_K), new_spec(COL_V), cache_spec, cache_spec,
                  pl.BlockSpec((N_HEADS, n_new, n_cache + n_new), lambda b: (0, 0, 0))],
        out_specs=pl.BlockSpec((n_new, D_ATTN), lambda b: (b, 0)),
        out_shape=jax.ShapeDtypeStruct((n_streams * n_new, D_ATTN), BF16),
        compiler_params=pltpu.CompilerParams(
            dimension_semantics=("arbitrary",), vmem_limit_bytes=VMEM_LIMIT),
        name="attn_sample",
    )(hact, hact, hact, cache_k, cache_v, bias)


def _merge_kernel(x_ref, u_ref, vb_ref, ga0_ref, ga1_ref, gb0_ref, gb1_ref, a_ref,
                  wbd_ref, sb_ref, wa_ref, wb_ref, wo_ref, o_ref, s_ref):
    tm = x_ref.shape[0]
    for c in range(tm // SGU_SUB):
        rows = slice(c * SGU_SUB, (c + 1) * SGU_SUB)
        for g in range(N_GROUPS):
            cols = slice(g * GROUP_DIM, (g + 1) * GROUP_DIM)
            mixed = _dot(wbd_ref[g], vb_ref[rows, cols]) + sb_ref[:, cols]
            s_ref[rows, cols] = (u_ref[rows, cols].astype(F32) * mixed).astype(BF16)
    pa = _dot(a_ref[...], wa_ref[...])
    pb = _dot(s_ref[...], wb_ref[...])
    half = D_MODEL // 2
    m0 = ga0_ref[...].astype(F32) * pa[:, :half] + gb0_ref[...].astype(F32) * pb[:, :half]
    m1 = ga1_ref[...].astype(F32) * pa[:, half:] + gb1_ref[...].astype(F32) * pb[:, half:]
    m = jnp.concatenate([m0, m1], axis=-1).astype(BF16)
    o_ref[...] = x_ref[...] + _dot(m, wo_ref[...])


def _sgu_block_weights(w_s, b_s, chunk):
    reps = SGU_SUB // chunk
    w = (w_s * jnp.tril(jnp.ones((SGU_CHUNK, SGU_CHUNK), w_s.dtype)))[:, :chunk, :chunk]
    eye = jnp.eye(reps, dtype=w.dtype)
    wbd = (eye[None, :, None, :, None] * w[:, None, :, None, :]).reshape(N_GROUPS, SGU_SUB, SGU_SUB)
    bias = jnp.repeat(jnp.tile(b_s[:, :chunk].T, (reps, 1)), GROUP_DIM, axis=1)
    return wbd.astype(BF16), bias.astype(F32)


def _merge(x, hact, a, wbd, sbias, wa, wb, wo, *, tm):
    m = x.shape[0]
    hcol = lambda col: pl.BlockSpec((tm, COL), lambda i: (i, col))
    const = lambda shape: pl.BlockSpec(shape, lambda i: (0,) * len(shape), pipeline_mode=pl.Buffered(1))
    return pl.pallas_call(
        _merge_kernel,
        grid=(m // tm,),
        in_specs=[pl.BlockSpec((tm, D_MODEL), lambda i: (i, 0)),
                  hcol(COL_U), hcol(COL_VB), hcol(COL_GA), hcol(COL_GA + 1), hcol(COL_GB), hcol(COL_GB + 1),
                  pl.BlockSpec((tm, D_ATTN), lambda i: (i, 0)),
                  const((N_GROUPS, SGU_SUB, SGU_SUB)), const((SGU_SUB, D_SGU)),
                  const((D_ATTN, D_MODEL)), const((D_SGU, D_MODEL)), const((D_MODEL, D_MODEL))],
        out_specs=pl.BlockSpec((tm, D_MODEL), lambda i: (i, 0)),
        out_shape=jax.ShapeDtypeStruct((m, D_MODEL), F32),
        scratch_shapes=[pltpu.VMEM((tm, D_SGU), BF16)],
        compiler_params=pltpu.CompilerParams(
            dimension_semantics=("arbitrary",), vmem_limit_bytes=VMEM_LIMIT),
        name="merge",
    )(x, hact, hact, hact, hact, hact, hact, a, wbd, sbias, wa, wb, wo)


def _conv_gate(hg, hv, prev_g, prev_v, cwg_ref, cwv_ref, cbg_ref, cbv_ref):
    def conv(h, prev, cw_ref, cb_ref):
        return cb_ref[...] + cw_ref[0:1, :] * prev(2) + cw_ref[1:2, :] * prev(1) + cw_ref[2:3, :] * h
    return (_gelu(conv(hg, prev_g, cwg_ref, cbg_ref)) * conv(hv, prev_v, cwv_ref, cbv_ref)).astype(BF16)


def _ffn_prologue(j, x_ref, g_ref, y_ref, xn_ref):
    @pl.when(j == 0)
    def _():
        x = x_ref[...]
        xn_ref[...] = _rms(x, g_ref[...]).astype(BF16)
        y_ref[...] = x


def _ffn_epilogue(j, gf_ref, y_ref):
    @pl.when(j == N_FF_TILES - 1)
    def _():
        y_ref[...] = _rms(y_ref[...], gf_ref[...])


def _ffn_prompt_kernel(x_ref, g_ref, wg_ref, wv_ref, cwg_ref, cwv_ref, cbg_ref, cbv_ref, wd_ref, gf_ref,
                       y_ref, hlg_ref, hlv_ref, xn_ref, cg_ref, cv_ref, *, blocks_per_seq):
    i = pl.program_id(0)
    j = pl.program_id(1)
    tm = x_ref.shape[0]

    @pl.when(i % blocks_per_seq == 0)
    def _():
        cg_ref[j] = jnp.zeros(cg_ref.shape[1:], F32)
        cv_ref[j] = jnp.zeros(cv_ref.shape[1:], F32)

    def delayed(h, carry):
        def prev(k):
            head = jnp.concatenate([carry, h[:CARRY_ROWS]], axis=0)[CARRY_ROWS - k:2 * CARRY_ROWS - k]
            return jnp.concatenate([head, pltpu.roll(h, k, axis=0)[CARRY_ROWS:]], axis=0)
        return prev

    def step(first, last):
        if first:
            xn_ref[...] = _rms(x_ref[...], g_ref[...]).astype(BF16)
        hg = _dot(xn_ref[...], wg_ref[...])
        hv = _dot(xn_ref[...], wv_ref[...])
        act = _conv_gate(hg, hv, delayed(hg, cg_ref[j]), delayed(hv, cv_ref[j]), cwg_ref, cwv_ref, cbg_ref, cbv_ref)
        y = (x_ref if first else y_ref)[...] + _dot(act, wd_ref[...])
        y_ref[...] = _rms(y, gf_ref[...]) if last else y
        last_g, last_v = hg[tm - CARRY_ROWS:], hv[tm - CARRY_ROWS:]
        cg_ref[j] = last_g
        cv_ref[j] = last_v
        hlg_ref[0] = last_g
        hlv_ref[0] = last_v

    @pl.when(j == 0)
    def _():
        step(True, False)

    @pl.when((j > 0) & (j < N_FF_TILES - 1))
    def _():
        step(False, False)

    @pl.when(j == N_FF_TILES - 1)
    def _():
        step(False, True)


def _ffn_sample_kernel(x_ref, g_ref, wg_ref, wv_ref, cwg_ref, cwv_ref, cbg_ref, cbv_ref, wd_ref, gf_ref,
                       pg_ref, pv_ref, y_ref, hg_ref, hv_ref, xn_ref, *, seq):
    j = pl.program_id(1)
    tm = x_ref.shape[0]
    _ffn_prologue(j, x_ref, g_ref, y_ref, xn_ref)
    hg = _dot(xn_ref[...], wg_ref[...])
    hv = _dot(xn_ref[...], wv_ref[...])
    hg_ref[...] = hg
    hv_ref[...] = hv
    pos = lax.broadcasted_iota(jnp.int32, (tm, FF_TILE), 0) % seq

    def delayed(h, hist_ref):
        def expand(t):
            n = hist_ref.shape[0]
            return jnp.broadcast_to(hist_ref[:, t:t + 1, :], (n, seq, FF_TILE)).reshape(tm, FF_TILE)

        def prev(k):
            rolled = pltpu.roll(h, k, axis=0)
            if k == 1:
                return jnp.where(pos == 0, expand(1), rolled)
            return jnp.where(pos == 0, expand(0), jnp.where(pos == 1, expand(1), rolled))
        return prev

    act = _conv_gate(hg, hv, delayed(hg, pg_ref), delayed(hv, pv_ref), cwg_ref, cwv_ref, cbg_ref, cbv_ref)
    y_ref[...] += _dot(act, wd_ref[...])
    _ffn_epilogue(j, gf_ref, y_ref)


def _ffn_common_specs(tm):
    row = lambda i, j: (i, 0)
    fixed = lambda i, j: (0, 0)
    gate = lambda i, j: (0, j)
    val = lambda i, j: (0, N_FF_TILES + j)
    return [
        pl.BlockSpec((tm, D_MODEL), row),
        pl.BlockSpec((1, D_MODEL), fixed),
        pl.BlockSpec((D_MODEL, FF_TILE), gate),
        pl.BlockSpec((D_MODEL, FF_TILE), val),
        pl.BlockSpec((CONV_W, FF_TILE), gate),
        pl.BlockSpec((CONV_W, FF_TILE), val),
        pl.BlockSpec((1, FF_TILE), gate),
        pl.BlockSpec((1, FF_TILE), val),
        pl.BlockSpec((FF_TILE, D_MODEL), lambda i, j: (j, 0)),
        pl.BlockSpec((1, D_MODEL), fixed),
    ]


def _ffn_prompt(x, g, w_up, conv_w, conv_b, w_down, gf, *, tm, rows_per_seq):
    m = x.shape[0]
    blocks_per_seq = rows_per_seq // tm
    last_spec = pl.BlockSpec((1, CARRY_ROWS, FF_TILE), lambda i, j: (i, 0, j))
    last_shape = jax.ShapeDtypeStruct((m // tm, CARRY_ROWS, D_FF), F32)
    carry = pltpu.VMEM((N_FF_TILES, CARRY_ROWS, FF_TILE), F32)
    return pl.pallas_call(
        functools.partial(_ffn_prompt_kernel, blocks_per_seq=blocks_per_seq),
        grid=(m // tm, N_FF_TILES),
        in_specs=_ffn_common_specs(tm),
        out_specs=[pl.BlockSpec((tm, D_MODEL), lambda i, j: (i, 0)), last_spec, last_spec],
        out_shape=[jax.ShapeDtypeStruct((m, D_MODEL), F32), last_shape, last_shape],
        scratch_shapes=[pltpu.VMEM((tm, D_MODEL), BF16), carry, carry],
        compiler_params=pltpu.CompilerParams(
            dimension_semantics=("arbitrary", "arbitrary"), vmem_limit_bytes=VMEM_LIMIT_FFN),
        name="ffn_prompt",
    )(x, g, w_up, w_up, conv_w, conv_w, conv_b, conv_b, w_down, gf)


def _ffn_sample(x, g, w_up, conv_w, conv_b, w_down, gf, hist, *, seq):
    m = x.shape[0]
    n_streams = m // seq
    hist_g = pl.BlockSpec((n_streams, CONV_W - 1, FF_TILE), lambda i, j: (0, 0, j))
    hist_v = pl.BlockSpec((n_streams, CONV_W - 1, FF_TILE), lambda i, j: (0, 0, N_FF_TILES + j))
    h_spec = pl.BlockSpec((m, FF_TILE), lambda i, j: (0, j))
    h_shape = jax.ShapeDtypeStruct((m, D_FF), F32)
    return pl.pallas_call(
        functools.partial(_ffn_sample_kernel, seq=seq),
        grid=(1, N_FF_TILES),
        in_specs=_ffn_common_specs(m) + [hist_g, hist_v],
        out_specs=[pl.BlockSpec((m, D_MODEL), lambda i, j: (0, 0)), h_spec, h_spec],
        out_shape=[jax.ShapeDtypeStruct((m, D_MODEL), F32), h_shape, h_shape],
        scratch_shapes=[pltpu.VMEM((m, D_MODEL), BF16)],
        compiler_params=pltpu.CompilerParams(
            dimension_semantics=("arbitrary", "arbitrary"), vmem_limit_bytes=VMEM_LIMIT),
        name="ffn_sample",
    )(x, g, w_up, w_up, conv_w, conv_w, conv_b, conv_b, w_down, gf, hist, hist)


def kernel(x_prompt, x_sample, cache_k, cache_v, cache_ffn_conv, norm_mix_g, w_in, rel_bias, sgu_norm_g, w_s, b_s,
           w_branch_a, w_branch_b, w_out, norm_ffn_g, w_up, conv_w, conv_b, w_down, norm_final_g):
    depth = w_in.shape[0]
    assert depth == 1, "single-layer trunk"
    batch, seq, _ = x_prompt.shape
    n_streams, n_new, _ = x_sample.shape
    n_cache = cache_k.shape[2]
    keep = min(KV_WINDOW, seq)

    row = lambda v: v.reshape(1, -1).astype(F32)
    w_in_bf = w_in[0].astype(BF16)
    wa_bf, wb_bf, wo_bf = w_branch_a[0].astype(BF16), w_branch_b[0].astype(BF16), w_out[0].astype(BF16)
    w_up_bf, w_down_bf = w_up[0].astype(BF16), w_down[0].astype(BF16)
    g_mix, g_sgu, g_ffn, g_fin = row(norm_mix_g[0]), row(sgu_norm_g[0]), row(norm_ffn_g[0]), row(norm_final_g)
    cw, cb = conv_w[0].astype(F32), row(conv_b[0])

    xp = x_prompt.reshape(batch * seq, D_MODEL)
    hact_p, k_tail, v_tail = _in_proj(xp, g_mix, w_in_bf, g_sgu, tm=min(PROJ_ROWS, seq), rows_per_seq=seq,
                                      tail=keep, vb_tail=False)
    a_p = _attn_prompt(hact_p, _prompt_bias(rel_bias[0]), batch=batch, seq=seq)
    wbd_p, sb_p = _sgu_block_weights(w_s[0], b_s[0], SGU_CHUNK)
    x1_p = _merge(xp, hact_p, a_p, wbd_p, sb_p, wa_bf, wb_bf, wo_bf, tm=MERGE_ROWS)
    ffn_rows = min(FFN_ROWS, seq)
    y_p, hl_g, hl_v = _ffn_prompt(x1_p, g_ffn, w_up_bf, cw, cb, w_down_bf, g_fin, tm=ffn_rows, rows_per_seq=seq)

    ms = n_streams * n_new
    xs = x_sample.reshape(ms, D_MODEL)
    hact_s, k_new, v_new, vb_new = _in_proj(xs, g_mix, w_in_bf, g_sgu, tm=ms, rows_per_seq=ms, tail=ms,
                                            vb_tail=True)
    a_s = _attn_sample(hact_s, cache_k[0].reshape(n_streams, n_cache * N_HEADS, HEAD_DIM),
                       cache_v[0].reshape(n_streams, n_cache * N_HEADS, HEAD_DIM),
                       _sample_bias(rel_bias[0], n_cache, n_new), n_streams=n_streams, n_new=n_new)
    wbd_s, sb_s = _sgu_block_weights(w_s[0], b_s[0], n_new)
    x1_s = _merge(xs, hact_s, a_s, wbd_s, sb_s, wa_bf, wb_bf, wo_bf, tm=ms)
    y_s, h_g, h_v = _ffn_sample(x1_s, g_ffn, w_up_bf, cw, cb, w_down_bf, g_fin,
                                cache_ffn_conv[0], seq=n_new)

    hist = CONV_W - 1
    heads = lambda t, b, s: t.reshape(1, b, s, N_HEADS, HEAD_DIM)
    seq_end = slice(seq // ffn_rows - 1, None, seq // ffn_rows)
    new_conv_prompt = jnp.concatenate(
        [hl_g[seq_end, CARRY_ROWS - hist:], hl_v[seq_end, CARRY_ROWS - hist:]], axis=-1)[None]
    h_s = jnp.concatenate([h_g, h_v], axis=-1).reshape(n_streams, n_new, 2 * D_FF)
    new_conv_sample = h_s[:, n_new - hist:][None]
    return (
        y_p.reshape(batch, seq, D_MODEL),
        y_s.reshape(n_streams, n_new, D_MODEL),
        heads(k_tail, batch, keep),
        heads(v_tail, batch, keep),
        heads(k_new, n_streams, n_new),
        heads(v_new, n_streams, n_new),
        vb_new.reshape(1, n_streams, n_new, D_SGU),
        new_conv_prompt,
        new_conv_sample,
    )
```

```python
import functools

import numpy as np
import jax
import jax.numpy as jnp
from jax import lax
from jax.experimental import pallas as pl
from jax.experimental.pallas import tpu as pltpu

D_MODEL = 2048
CHUNK = 64
N_LEFT_CHUNKS = 8
KV_WINDOW = N_LEFT_CHUNKS * CHUNK
D_ATTN = D_MODEL // 2
N_HEADS = 8
HEAD_DIM = D_ATTN // N_HEADS
MAX_REL = 256
D_SGU = D_MODEL // 2
N_GROUPS = 8
GROUP_DIM = D_SGU // N_GROUPS
SGU_CHUNK = 128
D_FF = 5632
CONV_W = 3
EPS = 1e-6
PAST_LEN = 2048
D_IN = 3 * D_ATTN + 2 * D_SGU + 2 * D_MODEL
NEG_INF = -1e30

COL = 1024
COL_Q, COL_K, COL_V, COL_U, COL_VB, COL_GA, COL_GB = 0, 1, 2, 3, 4, 5, 7
N_COL_BLOCKS = D_IN // COL

PROJ_ROWS = 1024
MERGE_ROWS = 512
FFN_ROWS = 1024
FFN_SUB = 1024
PROJ_COL_CHUNK = 512
PROJ_ROW_CHUNK = 256
ATTN_QB = 256
ATTN_QSUB = 256
LOG2E = float(np.log2(np.e))
SGU_SUB = 256
FF_TILE = 512
N_FF_TILES = D_FF // FF_TILE
CARRY_ROWS = 8

VMEM_LIMIT = 56 * 1024 * 1024
VMEM_LIMIT_FFN = 60 * 1024 * 1024

BF16 = jnp.bfloat16
F32 = jnp.float32


def _rms(x, g):
    inv = lax.rsqrt(jnp.mean(x * x, axis=-1, keepdims=True) + EPS)
    return (x * inv) * g


def _gelu(x):
    return 0.5 * x * (1.0 + lax.erf(x * (2.0 ** -0.5)))


def _sigmoid(x):
    return 0.5 * jnp.tanh(0.5 * x) + 0.5


def _dot(a, b):
    return jnp.dot(a, b, preferred_element_type=F32)


def _dot_nt(a, b):
    return lax.dot_general(a, b, (((1,), (1,)), ((), ())), preferred_element_type=F32)


def _in_proj_kernel(x_ref, g_ref, w_ref, sg_ref, h_ref, kt_ref, vt_ref, *rest, tail):
    vbt_ref, xn_ref = rest if len(rest) == 2 else (None, rest[0])
    j = pl.program_id(1)
    tm = x_ref.shape[0]
    head = tm - tail

    @pl.when(j == 0)
    def _():
        xn_ref[...] = _rms(x_ref[...], g_ref[...]).astype(BF16)

    def by_cols(act, tail_ref=None):
        for c in range(COL // PROJ_COL_CHUNK):
            cols = slice(c * PROJ_COL_CHUNK, (c + 1) * PROJ_COL_CHUNK)
            acc = _dot(xn_ref[...], w_ref[:, cols])
            h_ref[:, cols] = act(acc).astype(BF16)
            if tail_ref is not None:
                tail_ref[:, cols] = acc[head:, :]

    @pl.when(j == COL_Q)
    def _():
        by_cols(lambda a: a)

    @pl.when(j == COL_K)
    def _():
        by_cols(lambda a: a, kt_ref)

    @pl.when(j == COL_V)
    def _():
        by_cols(lambda a: a, vt_ref)

    @pl.when(j == COL_U)
    def _():
        by_cols(_gelu)

    @pl.when(j == COL_VB)
    def _():
        for r in range(tm // PROJ_ROW_CHUNK):
            lo = r * PROJ_ROW_CHUNK
            rows = slice(lo, lo + PROJ_ROW_CHUNK)
            vb = _rms(_gelu(_dot(xn_ref[rows, :], w_ref[...])), sg_ref[...])
            h_ref[rows, :] = vb.astype(BF16)
            if vbt_ref is not None and lo >= head:
                vbt_ref[lo - head:lo - head + PROJ_ROW_CHUNK, :] = vb

    @pl.when(j >= COL_GA)
    def _():
        by_cols(_sigmoid)


def _in_proj(x, g, w_bf, sg, *, tm, rows_per_seq, tail, vb_tail):
    m = x.shape[0]
    blocks_per_seq = rows_per_seq // tm
    n_seq = m // rows_per_seq
    assert tail <= tm and (tm - tail) % PROJ_ROW_CHUNK == 0
    n_tails = 3 if vb_tail else 2
    tail_spec = pl.BlockSpec((tail, COL), lambda i, j: (i // blocks_per_seq, 0))
    tail_shape = jax.ShapeDtypeStruct((n_seq * tail, COL), F32)
    return pl.pallas_call(
        functools.partial(_in_proj_kernel, tail=tail),
        grid=(m // tm, N_COL_BLOCKS),
        in_specs=[
            pl.BlockSpec((tm, D_MODEL), lambda i, j: (i, 0)),
            pl.BlockSpec((1, D_MODEL), lambda i, j: (0, 0)),
            pl.BlockSpec((D_MODEL, COL), lambda i, j: (0, j)),
            pl.BlockSpec((1, D_SGU), lambda i, j: (0, 0)),
        ],
        out_specs=[pl.BlockSpec((tm, COL), lambda i, j: (i, j))] + [tail_spec] * n_tails,
        out_shape=[jax.ShapeDtypeStruct((m, D_IN), BF16)] + [tail_shape] * n_tails,
        scratch_shapes=[pltpu.VMEM((tm, D_MODEL), BF16)],
        compiler_params=pltpu.CompilerParams(
            dimension_semantics=("arbitrary", "arbitrary"), vmem_limit_bytes=VMEM_LIMIT),
        name="in_proj",
    )(x, g, w_bf, sg)


def _attn_prompt_kernel(q_ref, k0_ref, k1_ref, k2_ref, v0_ref, v1_ref, v2_ref, bias_ref, o_ref):
    qb = pl.program_id(1)
    k_refs = (k0_ref, k1_ref, k2_ref)
    v_refs = (v0_ref, v1_ref, v2_ref)
    scale2 = HEAD_DIM ** -0.5 * LOG2E

    ones = jnp.ones((3 * ATTN_QB, HEAD_DIM), BF16)

    def attend(mask_missing):
        for h in range(N_HEADS):
            cols = slice(h * HEAD_DIM, (h + 1) * HEAD_DIM)
            k = jnp.concatenate([r[:, cols] for r in k_refs], axis=0)
            v = jnp.concatenate([jnp.concatenate([r[:, cols] for r in v_refs], axis=0), ones], axis=1)
            s = _dot_nt(q_ref[:, cols], k) * scale2 + bias_ref[h]
            if mask_missing:
                s = jnp.concatenate(
                    [s[:, r * ATTN_QB:(r + 1) * ATTN_QB] + jnp.where(qb - 2 + r >= 0, 0.0, NEG_INF).astype(F32)
                     for r in range(3)], axis=1)
            e = jnp.exp2(s - s.max(-1, keepdims=True)).astype(BF16)
            o = _dot(e, v)
            o_ref[:, cols] = (o[:, :HEAD_DIM] / o[:, HEAD_DIM:]).astype(BF16)

    @pl.when(qb < KV_WINDOW // ATTN_QB)
    def _():
        attend(True)

    @pl.when(qb >= KV_WINDOW // ATTN_QB)
    def _():
        attend(False)


def _rel_bias_table(rel_bias, n, m, d0):
    length = n + m - 1
    dist = np.arange(length) - (m - 1) + d0
    diag = rel_bias[:, np.clip(dist, -MAX_REL, MAX_REL) + MAX_REL].astype(F32)
    rev = diag[:, ::-1]
    padded = jnp.concatenate([rev, rev[:, :1]], axis=1)
    skew = jnp.tile(padded, (1, n))[:, :n * length].reshape(-1, n, length)
    return skew[:, :, n - 1:n - 1 + m]


def _prompt_bias(rel_bias):
    qi = np.arange(ATTN_QB)[:, None]
    kp = np.arange(3 * ATTN_QB)[None, :] - 2 * ATTN_QB
    cq, ck = qi // CHUNK, np.floor_divide(kp, CHUNK)
    allowed = (ck <= cq) & (cq - ck <= N_LEFT_CHUNKS)
    bias = _rel_bias_table(rel_bias, ATTN_QB, 3 * ATTN_QB, 2 * ATTN_QB) * LOG2E
    return jnp.where(allowed[None], bias, NEG_INF)


def _attn_prompt(hact, bias, *, batch, seq):
    nqb = seq // ATTN_QB
    kv_spec = lambda col, r: pl.BlockSpec(
        (ATTN_QB, COL), lambda b, t: (b * nqb + jnp.maximum(t - 2 + r, 0), col))
    return pl.pallas_call(
        _attn_prompt_kernel,
        grid=(batch, nqb),
        in_specs=[pl.BlockSpec((ATTN_QB, COL), lambda b, t: (b * nqb + t, COL_Q))]
        + [kv_spec(COL_K, r) for r in range(3)] + [kv_spec(COL_V, r) for r in range(3)]
        + [pl.BlockSpec((N_HEADS, ATTN_QB, 3 * ATTN_QB), lambda b, t: (0, 0, 0))],
        out_specs=pl.BlockSpec((ATTN_QB, D_ATTN), lambda b, t: (b * nqb + t, 0)),
        out_shape=jax.ShapeDtypeStruct((batch * seq, D_ATTN), BF16),
        compiler_params=pltpu.CompilerParams(
            dimension_semantics=("arbitrary", "arbitrary"), vmem_limit_bytes=VMEM_LIMIT),
        name="attn_prompt",
    )(hact, hact, hact, hact, hact, hact, hact, bias)


def _attn_sample_kernel(q_ref, kn_ref, vn_ref, kc_ref, vc_ref, bias_ref, o_ref):
    n_cache = kc_ref.shape[1] // N_HEADS
    scale = HEAD_DIM ** -0.5
    for h in range(N_HEADS):
        cols = slice(h * HEAD_DIM, (h + 1) * HEAD_DIM)
        head_rows = pl.ds(h, n_cache, stride=N_HEADS)
        q = q_ref[:, cols]
        s_c = _dot_nt(q, kc_ref[0, head_rows, :].astype(BF16)) * scale + bias_ref[h, :, :n_cache]
        s_n = _dot_nt(q, kn_ref[:, cols]) * scale + bias_ref[h, :, n_cache:]
        mx = jnp.maximum(s_c.max(-1, keepdims=True), s_n.max(-1, keepdims=True))
        e_c, e_n = jnp.exp(s_c - mx), jnp.exp(s_n - mx)
        den = e_c.sum(-1, keepdims=True) + e_n.sum(-1, keepdims=True)
        o = (_dot(e_c.astype(BF16), vc_ref[0, head_rows, :].astype(BF16))
             + _dot(e_n.astype(BF16), vn_ref[:, cols]))
        o_ref[:, cols] = (o / den).astype(BF16)


def _sample_bias(rel_bias, n_cache, n_new):
    q_pos = PAST_LEN + np.arange(n_new)
    k_pos = np.concatenate([PAST_LEN - n_cache + np.arange(n_cache), PAST_LEN + np.arange(n_new)])
    cq, ck = q_pos[:, None] // CHUNK, k_pos[None, :] // CHUNK
    allowed = (ck <= cq) & (cq - ck <= N_LEFT_CHUNKS)
    bias = jnp.concatenate([_rel_bias_table(rel_bias, n_new, n_cache, n_cache),
                            _rel_bias_table(rel_bias, n_new, n_new, 0)], axis=2)
    return jnp.where(allowed[None], bias, NEG_INF)


def _attn_sample(hact, cache_k, cache_v, bias, *, n_streams, n_new):
    n_cache = cache_k.shape[1] // N_HEADS
    new_spec = lambda col: pl.BlockSpec((n_new, COL), lambda b: (b, col))
    cache_spec = pl.BlockSpec((1, n_cache * N_HEADS, HEAD_DIM), lambda b: (b, 0, 0))
    return pl.pallas_call(
        _attn_sample_kernel,
        grid=(n_streams,),
        in_specs=[new_spec(COL_Q), new_spec(COL_K), new_spec(COL_V), cache_spec, cache_spec,
                  pl.BlockSpec((N_HEADS, n_new, n_cache + n_new), lambda b: (0, 0, 0))],
        out_specs=pl.BlockSpec((n_new, D_ATTN), lambda b: (b, 0)),
        out_shape=jax.ShapeDtypeStruct((n_streams * n_new, D_ATTN), BF16),
        compiler_params=pltpu.CompilerParams(
            dimension_semantics=("arbitrary",), vmem_limit_bytes=VMEM_LIMIT),
        name="attn_sample",
    )(hact, hact, hact, cache_k, cache_v, bias)


def _merge_kernel(x_ref, u_ref, vb_ref, ga0_ref, ga1_ref, gb0_ref, gb1_ref, a_ref,
                  wbd_ref, sb_ref, wa_ref, wb_ref, wo_ref, o_ref, s_ref):
    tm = x_ref.shape[0]
    for c in range(tm // SGU_SUB):
        rows = slice(c * SGU_SUB, (c + 1) * SGU_SUB)
        for g in range(N_GROUPS):
            cols = slice(g * GROUP_DIM, (g + 1) * GROUP_DIM)
            mixed = _dot(wbd_ref[g], vb_ref[rows, cols]) + sb_ref[:, cols]
            s_ref[rows, cols] = (u_ref[rows, cols].astype(F32) * mixed).astype(BF16)
    pa = _dot(a_ref[...], wa_ref[...])
    pb = _dot(s_ref[...], wb_ref[...])
    half = D_MODEL // 2
    m0 = ga0_ref[...].astype(F32) * pa[:, :half] + gb0_ref[...].astype(F32) * pb[:, :half]
    m1 = ga1_ref[...].astype(F32) * pa[:, half:] + gb1_ref[...].astype(F32) * pb[:, half:]
    m = jnp.concatenate([m0, m1], axis=-1).astype(BF16)
    o_ref[...] = x_ref[...] + _dot(m, wo_ref[...])


def _sgu_block_weights(w_s, b_s, chunk):
    reps = SGU_SUB // chunk
    w = (w_s * jnp.tril(jnp.ones((SGU_CHUNK, SGU_CHUNK), w_s.dtype)))[:, :chunk, :chunk]
    eye = jnp.eye(reps, dtype=w.dtype)
    wbd = (eye[None, :, None, :, None] * w[:, None, :, None, :]).reshape(N_GROUPS, SGU_SUB, SGU_SUB)
    bias = jnp.repeat(jnp.tile(b_s[:, :chunk].T, (reps, 1)), GROUP_DIM, axis=1)
    return wbd.astype(BF16), bias.astype(F32)


def _merge(x, hact, a, wbd, sbias, wa, wb, wo, *, tm):
    m = x.shape[0]
    hcol = lambda col: pl.BlockSpec((tm, COL), lambda i: (i, col))
    const = lambda shape: pl.BlockSpec(shape, lambda i: (0,) * len(shape), pipeline_mode=pl.Buffered(1))
    return pl.pallas_call(
        _merge_kernel,
        grid=(m // tm,),
        in_specs=[pl.BlockSpec((tm, D_MODEL), lambda i: (i, 0)),
                  hcol(COL_U), hcol(COL_VB), hcol(COL_GA), hcol(COL_GA + 1), hcol(COL_GB), hcol(COL_GB + 1),
                  pl.BlockSpec((tm, D_ATTN), lambda i: (i, 0)),
                  const((N_GROUPS, SGU_SUB, SGU_SUB)), const((SGU_SUB, D_SGU)),
                  const((D_ATTN, D_MODEL)), const((D_SGU, D_MODEL)), const((D_MODEL, D_MODEL))],
        out_specs=pl.BlockSpec((tm, D_MODEL), lambda i: (i, 0)),
        out_shape=jax.ShapeDtypeStruct((m, D_MODEL), F32),
        scratch_shapes=[pltpu.VMEM((tm, D_SGU), BF16)],
        compiler_params=pltpu.CompilerParams(
            dimension_semantics=("arbitrary",), vmem_limit_bytes=VMEM_LIMIT),
        name="merge",
    )(x, hact, hact, hact, hact, hact, hact, a, wbd, sbias, wa, wb, wo)


def _conv_gate(hg, hv, prev_g, prev_v, cwg, cwv, cbg, cbv):
    def conv(h, prev, cw, cb):
        return cb + cw[0:1, :] * prev(2) + cw[1:2, :] * prev(1) + cw[2:3, :] * h
    return (_gelu(conv(hg, prev_g, cwg, cbg)) * conv(hv, prev_v, cwv, cbv)).astype(BF16)


def _ffn_prologue(j, x_ref, g_ref, y_ref, xn_ref):
    @pl.when(j == 0)
    def _():
        x = x_ref[...]
        xn_ref[...] = _rms(x, g_ref[...]).astype(BF16)
        y_ref[...] = x


def _ffn_epilogue(j, gf_ref, y_ref):
    @pl.when(j == N_FF_TILES - 1)
    def _():
        y_ref[...] = _rms(y_ref[...], gf_ref[...])


def _ffn_prompt_kernel(x_ref, g_ref, wg_ref, wv_ref, cw_ref, cb_ref, wd_ref, gf_ref,
                       y_ref, hl_ref, xn_ref, cg_ref, cv_ref, *, blocks_per_seq):
    i = pl.program_id(0)
    j = pl.program_id(1)
    tm = x_ref.shape[0]

    @pl.when(i % blocks_per_seq == 0)
    def _():
        cg_ref[j] = jnp.zeros(cg_ref.shape[1:], F32)
        cv_ref[j] = jnp.zeros(cv_ref.shape[1:], F32)

    def delayed(h, carry):
        def prev(k):
            head = jnp.concatenate([carry, h[:CARRY_ROWS]], axis=0)[CARRY_ROWS - k:2 * CARRY_ROWS - k]
            return jnp.concatenate([head, pltpu.roll(h, k, axis=0)[CARRY_ROWS:]], axis=0)
        return prev

    def step(first, last):
        if first:
            xn_ref[...] = _rms(x_ref[...], g_ref[...]).astype(BF16)
        hg = _dot(xn_ref[...], wg_ref[...])
        hv = _dot(xn_ref[...], wv_ref[...])
        act = _conv_gate(hg, hv, delayed(hg, cg_ref[j]), delayed(hv, cv_ref[j]),
                         cw_ref[j], cw_ref[N_FF_TILES + j], cb_ref[j], cb_ref[N_FF_TILES + j])
        y = (x_ref if first else y_ref)[...] + _dot(act, wd_ref[...])
        y_ref[...] = _rms(y, gf_ref[...]) if last else y
        last_g, last_v = hg[tm - CARRY_ROWS:], hv[tm - CARRY_ROWS:]
        cg_ref[j] = last_g
        cv_ref[j] = last_v
        hl_ref[0, 0, j] = last_g
        hl_ref[0, 1, j] = last_v

    @pl.when(j == 0)
    def _():
        step(True, False)

    @pl.when((j > 0) & (j < N_FF_TILES - 1))
    def _():
        step(False, False)

    @pl.when(j == N_FF_TILES - 1)
    def _():
        step(False, True)


def _ffn_sample_kernel(x_ref, g_ref, wg_ref, wv_ref, cwg_ref, cwv_ref, cbg_ref, cbv_ref, wd_ref, gf_ref,
                       pg_ref, pv_ref, y_ref, hg_ref, hv_ref, xn_ref, *, seq):
    j = pl.program_id(1)
    tm = x_ref.shape[0]
    _ffn_prologue(j, x_ref, g_ref, y_ref, xn_ref)
    hg = _dot(xn_ref[...], wg_ref[...])
    hv = _dot(xn_ref[...], wv_ref[...])
    hg_ref[...] = hg
    hv_ref[...] = hv
    pos = lax.broadcasted_iota(jnp.int32, (tm, FF_TILE), 0) % seq

    def delayed(h, hist_ref):
        def expand(t):
            n = hist_ref.shape[0]
            return jnp.broadcast_to(hist_ref[:, t:t + 1, :], (n, seq, FF_TILE)).reshape(tm, FF_TILE)

        def prev(k):
            rolled = pltpu.roll(h, k, axis=0)
            if k == 1:
                return jnp.where(pos == 0, expand(1), rolled)
            return jnp.where(pos == 0, expand(0), jnp.where(pos == 1, expand(1), rolled))
        return prev

    act = _conv_gate(hg, hv, delayed(hg, pg_ref), delayed(hv, pv_ref),
                     cwg_ref[...], cwv_ref[...], cbg_ref[...], cbv_ref[...])
    y_ref[...] += _dot(act, wd_ref[...])
    _ffn_epilogue(j, gf_ref, y_ref)


def _ffn_common_specs(tm):
    row = lambda i, j: (i, 0)
    fixed = lambda i, j: (0, 0)
    gate = lambda i, j: (0, j)
    val = lambda i, j: (0, N_FF_TILES + j)
    return [
        pl.BlockSpec((tm, D_MODEL), row),
        pl.BlockSpec((1, D_MODEL), fixed),
        pl.BlockSpec((D_MODEL, FF_TILE), gate),
        pl.BlockSpec((D_MODEL, FF_TILE), val),
        pl.BlockSpec((CONV_W, FF_TILE), gate),
        pl.BlockSpec((CONV_W, FF_TILE), val),
        pl.BlockSpec((1, FF_TILE), gate),
        pl.BlockSpec((1, FF_TILE), val),
        pl.BlockSpec((FF_TILE, D_MODEL), lambda i, j: (j, 0)),
        pl.BlockSpec((1, D_MODEL), fixed),
    ]


def _ffn_prompt(x, g, w_up, conv_w, conv_b, w_down, gf, *, tm, rows_per_seq):
    m = x.shape[0]
    blocks_per_seq = rows_per_seq // tm
    n_tiles = 2 * N_FF_TILES
    cw_tiles = conv_w.reshape(CONV_W, n_tiles, FF_TILE).transpose(1, 0, 2)
    cb_tiles = conv_b.reshape(n_tiles, 1, FF_TILE)
    row = lambda i, j: (i, 0)
    fixed2 = lambda i, j: (0, 0)
    fixed3 = lambda i, j: (0, 0, 0)
    last_shape = (1, 2, N_FF_TILES, CARRY_ROWS, FF_TILE)
    carry = pltpu.VMEM((N_FF_TILES, CARRY_ROWS, FF_TILE), F32)
    y, h_last = pl.pallas_call(
        functools.partial(_ffn_prompt_kernel, blocks_per_seq=blocks_per_seq),
        grid=(m // tm, N_FF_TILES),
        in_specs=[
            pl.BlockSpec((tm, D_MODEL), row),
            pl.BlockSpec((1, D_MODEL), fixed2),
            pl.BlockSpec((D_MODEL, FF_TILE), lambda i, j: (0, j)),
            pl.BlockSpec((D_MODEL, FF_TILE), lambda i, j: (0, N_FF_TILES + j)),
            pl.BlockSpec((n_tiles, CONV_W, FF_TILE), fixed3),
            pl.BlockSpec((n_tiles, 1, FF_TILE), fixed3),
            pl.BlockSpec((FF_TILE, D_MODEL), lambda i, j: (j, 0)),
            pl.BlockSpec((1, D_MODEL), fixed2),
        ],
        out_specs=[pl.BlockSpec((tm, D_MODEL), row), pl.BlockSpec(last_shape, lambda i, j: (i, 0, 0, 0, 0))],
        out_shape=[jax.ShapeDtypeStruct((m, D_MODEL), F32),
                   jax.ShapeDtypeStruct((m // tm,) + last_shape[1:], F32)],
        scratch_shapes=[pltpu.VMEM((tm, D_MODEL), BF16), carry, carry],
        compiler_params=pltpu.CompilerParams(
            dimension_semantics=("arbitrary", "arbitrary"), vmem_limit_bytes=VMEM_LIMIT_FFN),
        name="ffn_prompt",
    )(x, g, w_up, w_up, cw_tiles, cb_tiles, w_down, gf)
    tail = h_last[blocks_per_seq - 1::blocks_per_seq, :, :, CARRY_ROWS - (CONV_W - 1):, :]
    return y, tail.transpose(0, 3, 1, 2, 4).reshape(tail.shape[0], CONV_W - 1, 2 * D_FF)


def _ffn_sample(x, g, w_up, conv_w, conv_b, w_down, gf, hist, *, seq):
    m = x.shape[0]
    n_streams = m // seq
    hist_g = pl.BlockSpec((n_streams, CONV_W - 1, FF_TILE), lambda i, j: (0, 0, j))
    hist_v = pl.BlockSpec((n_streams, CONV_W - 1, FF_TILE), lambda i, j: (0, 0, N_FF_TILES + j))
    h_spec = pl.BlockSpec((m, FF_TILE), lambda i, j: (0, j))
    h_shape = jax.ShapeDtypeStruct((m, D_FF), F32)
    return pl.pallas_call(
        functools.partial(_ffn_sample_kernel, seq=seq),
        grid=(1, N_FF_TILES),
        in_specs=_ffn_common_specs(m) + [hist_g, hist_v],
        out_specs=[pl.BlockSpec((m, D_MODEL), lambda i, j: (0, 0)), h_spec, h_spec],
        out_shape=[jax.ShapeDtypeStruct((m, D_MODEL), F32), h_shape, h_shape],
        scratch_shapes=[pltpu.VMEM((m, D_MODEL), BF16)],
        compiler_params=pltpu.CompilerParams(
            dimension_semantics=("arbitrary", "arbitrary"), vmem_limit_bytes=VMEM_LIMIT),
        name="ffn_sample",
    )(x, g, w_up, w_up, conv_w, conv_w, conv_b, conv_b, w_down, gf, hist, hist)


def kernel(x_prompt, x_sample, cache_k, cache_v, cache_ffn_conv, norm_mix_g, w_in, rel_bias, sgu_norm_g, w_s, b_s,
           w_branch_a, w_branch_b, w_out, norm_ffn_g, w_up, conv_w, conv_b, w_down, norm_final_g):
    depth = w_in.shape[0]
    assert depth == 1, "single-layer trunk"
    batch, seq, _ = x_prompt.shape
    n_streams, n_new, _ = x_sample.shape
    n_cache = cache_k.shape[2]
    keep = min(KV_WINDOW, seq)

    row = lambda v: v.reshape(1, -1).astype(F32)
    w_in_bf = w_in[0].astype(BF16)
    wa_bf, wb_bf, wo_bf = w_branch_a[0].astype(BF16), w_branch_b[0].astype(BF16), w_out[0].astype(BF16)
    w_up_bf, w_down_bf = w_up[0].astype(BF16), w_down[0].astype(BF16)
    g_mix, g_sgu, g_ffn, g_fin = row(norm_mix_g[0]), row(sgu_norm_g[0]), row(norm_ffn_g[0]), row(norm_final_g)
    cw, cb = conv_w[0].astype(F32), row(conv_b[0])

    xp = x_prompt.reshape(batch * seq, D_MODEL)
    hact_p, k_tail, v_tail = _in_proj(xp, g_mix, w_in_bf, g_sgu, tm=min(PROJ_ROWS, seq), rows_per_seq=seq,
                                      tail=keep, vb_tail=False)
    a_p = _attn_prompt(hact_p, _prompt_bias(rel_bias[0]), batch=batch, seq=seq)
    wbd_p, sb_p = _sgu_block_weights(w_s[0], b_s[0], SGU_CHUNK)
    x1_p = _merge(xp, hact_p, a_p, wbd_p, sb_p, wa_bf, wb_bf, wo_bf, tm=MERGE_ROWS)
    ffn_rows = min(FFN_ROWS, seq)
    y_p, conv_p = _ffn_prompt(x1_p, g_ffn, w_up_bf, cw, cb, w_down_bf, g_fin, tm=ffn_rows, rows_per_seq=seq)

    ms = n_streams * n_new
    xs = x_sample.reshape(ms, D_MODEL)
    hact_s, k_new, v_new, vb_new = _in_proj(xs, g_mix, w_in_bf, g_sgu, tm=ms, rows_per_seq=ms, tail=ms,
                                            vb_tail=True)
    a_s = _attn_sample(hact_s, cache_k[0].reshape(n_streams, n_cache * N_HEADS, HEAD_DIM),
                       cache_v[0].reshape(n_streams, n_cache * N_HEADS, HEAD_DIM),
                       _sample_bias(rel_bias[0], n_cache, n_new), n_streams=n_streams, n_new=n_new)
    wbd_s, sb_s = _sgu_block_weights(w_s[0], b_s[0], n_new)
    x1_s = _merge(xs, hact_s, a_s, wbd_s, sb_s, wa_bf, wb_bf, wo_bf, tm=ms)
    y_s, h_g, h_v = _ffn_sample(x1_s, g_ffn, w_up_bf, cw, cb, w_down_bf, g_fin,
                                cache_ffn_conv[0], seq=n_new)

    hist = CONV_W - 1
    heads = lambda t, b, s: t.reshape(1, b, s, N_HEADS, HEAD_DIM)
    new_conv_prompt = conv_p[None]
    h_s = jnp.concatenate([h_g, h_v], axis=-1).reshape(n_streams, n_new, 2 * D_FF)
    new_conv_sample = h_s[:, n_new - hist:][None]
    return (
        y_p.reshape(batch, seq, D_MODEL),
        y_s.reshape(n_streams, n_new, D_MODEL),
        heads(k_tail, batch, keep),
        heads(v_tail, batch, keep),
        heads(k_new, n_streams, n_new),
        heads(v_new, n_streams, n_new),
        vb_new.reshape(1, n_streams, n_new, D_SGU),
        new_conv_prompt,
        new_conv_sample,
    )
```

```python
import functools

import numpy as np
import jax
import jax.numpy as jnp
from jax import lax
from jax.experimental import pallas as pl
from jax.experimental.pallas import tpu as pltpu

D_MODEL = 2048
CHUNK = 64
N_LEFT_CHUNKS = 8
KV_WINDOW = N_LEFT_CHUNKS * CHUNK
D_ATTN = D_MODEL // 2
N_HEADS = 8
HEAD_DIM = D_ATTN // N_HEADS
MAX_REL = 256
D_SGU = D_MODEL // 2
N_GROUPS = 8
GROUP_DIM = D_SGU // N_GROUPS
SGU_CHUNK = 128
D_FF = 5632
CONV_W = 3
EPS = 1e-6
PAST_LEN = 2048
D_IN = 3 * D_ATTN + 2 * D_SGU + 2 * D_MODEL
NEG_INF = -1e30

COL = 1024
COL_Q, COL_K, COL_V, COL_U, COL_VB, COL_GA, COL_GB = 0, 1, 2, 3, 4, 5, 7
N_COL_BLOCKS = D_IN // COL

PROJ_ROWS = 1024
MERGE_ROWS = 512
FFN_ROWS = 1024
FFN_SUB = 1024
PROJ_COL_CHUNK = 512
PROJ_ROW_CHUNK = 256
ATTN_QB = 256
ATTN_QSUB = 256
LOG2E = float(np.log2(np.e))
SGU_SUB = 256
FF_TILE = 512
N_FF_TILES = D_FF // FF_TILE
CARRY_ROWS = 8
BF16_SUBLANES = 16

VMEM_LIMIT = 56 * 1024 * 1024
VMEM_LIMIT_FFN = 60 * 1024 * 1024

BF16 = jnp.bfloat16
F32 = jnp.float32


def _rms(x, g):
    inv = lax.rsqrt(jnp.mean(x * x, axis=-1, keepdims=True) + EPS)
    return (x * inv) * g


def _gelu(x):
    return 0.5 * x * (1.0 + lax.erf(x * (2.0 ** -0.5)))


def _sigmoid(x):
    return 0.5 * jnp.tanh(0.5 * x) + 0.5


def _dot(a, b):
    return jnp.dot(a, b, preferred_element_type=F32)


def _dot_nt(a, b):
    return lax.dot_general(a, b, (((1,), (1,)), ((), ())), preferred_element_type=F32)


def _in_proj_kernel(x_ref, g_ref, w_ref, sg_ref, *refs, tail, vb_tail, n_cast):
    cast_in, refs = refs[:n_cast], refs[n_cast:]
    h_ref, kt_ref, vt_ref = refs[:3]
    vbt_ref = refs[3] if vb_tail else None
    cast_out, xn_ref = refs[3 + vb_tail:-1], refs[-1]
    j = pl.program_id(1)
    tm = x_ref.shape[0]
    head = tm - tail

    @pl.when(j == 0)
    def _():
        xn_ref[...] = _rms(x_ref[...], g_ref[...]).astype(BF16)

    def side_cast():
        for src, dst in zip(cast_in, cast_out):
            dst[...] = src[...].astype(BF16)

    def by_cols(act, tail_ref=None):
        side_cast()
        for c in range(COL // PROJ_COL_CHUNK):
            cols = slice(c * PROJ_COL_CHUNK, (c + 1) * PROJ_COL_CHUNK)
            acc = _dot(xn_ref[...], w_ref[:, cols])
            h_ref[:, cols] = act(acc).astype(BF16)
            if tail_ref is not None:
                tail_ref[:, cols] = acc[head:, :]

    @pl.when(j == COL_Q)
    def _():
        by_cols(lambda a: a)

    @pl.when(j == COL_K)
    def _():
        by_cols(lambda a: a, kt_ref)

    @pl.when(j == COL_V)
    def _():
        by_cols(lambda a: a, vt_ref)

    @pl.when(j == COL_U)
    def _():
        by_cols(_gelu)

    @pl.when(j == COL_VB)
    def _():
        side_cast()
        for r in range(tm // PROJ_ROW_CHUNK):
            lo = r * PROJ_ROW_CHUNK
            rows = slice(lo, lo + PROJ_ROW_CHUNK)
            vb = _rms(_gelu(_dot(xn_ref[rows, :], w_ref[...])), sg_ref[...])
            h_ref[rows, :] = vb.astype(BF16)
            if vbt_ref is not None and lo >= head:
                vbt_ref[lo - head:lo - head + PROJ_ROW_CHUNK, :] = vb

    @pl.when(j >= COL_GA)
    def _():
        by_cols(_sigmoid)


def _in_proj(x, g, w_bf, sg, *, tm, rows_per_seq, tail, vb_tail, cast=()):
    m = x.shape[0]
    blocks_per_seq = rows_per_seq // tm
    n_seq = m // rows_per_seq
    assert tail <= tm and (tm - tail) % PROJ_ROW_CHUNK == 0
    n_tails = 3 if vb_tail else 2
    n_steps = (m // tm) * N_COL_BLOCKS
    tail_spec = pl.BlockSpec((tail, COL), lambda i, j: (i // blocks_per_seq, 0))
    tail_shape = jax.ShapeDtypeStruct((n_seq * tail, COL), F32)

    def slab_spec(w):
        rows = BF16_SUBLANES
        while w.shape[0] // rows > n_steps:
            rows *= 2
        assert w.shape[0] % rows == 0
        last = w.shape[0] // rows - 1
        return pl.BlockSpec((rows, w.shape[1]), lambda i, j: (jnp.minimum(i * N_COL_BLOCKS + j, last), 0))

    slab_specs = [slab_spec(w) for w in cast]
    return pl.pallas_call(
        functools.partial(_in_proj_kernel, tail=tail, vb_tail=vb_tail, n_cast=len(cast)),
        grid=(m // tm, N_COL_BLOCKS),
        in_specs=[
            pl.BlockSpec((tm, D_MODEL), lambda i, j: (i, 0)),
            pl.BlockSpec((1, D_MODEL), lambda i, j: (0, 0)),
            pl.BlockSpec((D_MODEL, COL), lambda i, j: (0, j)),
            pl.BlockSpec((1, D_SGU), lambda i, j: (0, 0)),
        ] + slab_specs,
        out_specs=[pl.BlockSpec((tm, COL), lambda i, j: (i, j))] + [tail_spec] * n_tails + slab_specs,
        out_shape=[jax.ShapeDtypeStruct((m, D_IN), BF16)] + [tail_shape] * n_tails
        + [jax.ShapeDtypeStruct(w.shape, BF16) for w in cast],
        scratch_shapes=[pltpu.VMEM((tm, D_MODEL), BF16)],
        compiler_params=pltpu.CompilerParams(
            dimension_semantics=("arbitrary", "arbitrary"), vmem_limit_bytes=VMEM_LIMIT),
        name="in_proj",
    )(x, g, w_bf, sg, *cast)


def _attn_prompt_kernel(q_ref, k0_ref, k1_ref, k2_ref, v0_ref, v1_ref, v2_ref, bias_ref, o_ref):
    qb = pl.program_id(1)
    k_refs = (k0_ref, k1_ref, k2_ref)
    v_refs = (v0_ref, v1_ref, v2_ref)
    scale2 = HEAD_DIM ** -0.5 * LOG2E

    ones = jnp.ones((3 * ATTN_QB, HEAD_DIM), BF16)

    def attend(mask_missing):
        for h in range(N_HEADS):
            cols = slice(h * HEAD_DIM, (h + 1) * HEAD_DIM)
            k = jnp.concatenate([r[:, cols] for r in k_refs], axis=0)
            v = jnp.concatenate([jnp.concatenate([r[:, cols] for r in v_refs], axis=0), ones], axis=1)
            s = _dot_nt(q_ref[:, cols], k) * scale2 + bias_ref[h]
            if mask_missing:
                s = jnp.concatenate(
                    [s[:, r * ATTN_QB:(r + 1) * ATTN_QB] + jnp.where(qb - 2 + r >= 0, 0.0, NEG_INF).astype(F32)
                     for r in range(3)], axis=1)
            e = jnp.exp2(s - s.max(-1, keepdims=True)).astype(BF16)
            o = _dot(e, v)
            o_ref[:, cols] = (o[:, :HEAD_DIM] / o[:, HEAD_DIM:]).astype(BF16)

    @pl.when(qb < KV_WINDOW // ATTN_QB)
    def _():
        attend(True)

    @pl.when(qb >= KV_WINDOW // ATTN_QB)
    def _():
        attend(False)


def _rel_bias_table(rel_bias, n, m, d0):
    length = n + m - 1
    dist = np.arange(length) - (m - 1) + d0
    diag = rel_bias[:, np.clip(dist, -MAX_REL, MAX_REL) + MAX_REL].astype(F32)
    rev = diag[:, ::-1]
    padded = jnp.concatenate([rev, rev[:, :1]], axis=1)
    skew = jnp.tile(padded, (1, n))[:, :n * length].reshape(-1, n, length)
    return skew[:, :, n - 1:n - 1 + m]


def _prompt_bias(rel_bias):
    qi = np.arange(ATTN_QB)[:, None]
    kp = np.arange(3 * ATTN_QB)[None, :] - 2 * ATTN_QB
    cq, ck = qi // CHUNK, np.floor_divide(kp, CHUNK)
    allowed = (ck <= cq) & (cq - ck <= N_LEFT_CHUNKS)
    bias = _rel_bias_table(rel_bias, ATTN_QB, 3 * ATTN_QB, 2 * ATTN_QB) * LOG2E
    return jnp.where(allowed[None], bias, NEG_INF)


def _attn_prompt(hact, bias, *, batch, seq):
    nqb = seq // ATTN_QB
    kv_spec = lambda col, r: pl.BlockSpec(
        (ATTN_QB, COL), lambda b, t: (b * nqb + jnp.maximum(t - 2 + r, 0), col))
    return pl.pallas_call(
        _attn_prompt_kernel,
        grid=(batch, nqb),
        in_specs=[pl.BlockSpec((ATTN_QB, COL), lambda b, t: (b * nqb + t, COL_Q))]
        + [kv_spec(COL_K, r) for r in range(3)] + [kv_spec(COL_V, r) for r in range(3)]
        + [pl.BlockSpec((N_HEADS, ATTN_QB, 3 * ATTN_QB), lambda b, t: (0, 0, 0))],
        out_specs=pl.BlockSpec((ATTN_QB, D_ATTN), lambda b, t: (b * nqb + t, 0)),
        out_shape=jax.ShapeDtypeStruct((batch * seq, D_ATTN), BF16),
        compiler_params=pltpu.CompilerParams(
            dimension_semantics=("arbitrary", "arbitrary"), vmem_limit_bytes=VMEM_LIMIT),
        name="attn_prompt",
    )(hact, hact, hact, hact, hact, hact, hact, bias)


def _attn_sample_kernel(q_ref, kn_ref, vn_ref, kc_ref, vc_ref, bias_ref, o_ref):
    n_cache = kc_ref.shape[1] // N_HEADS
    scale = HEAD_DIM ** -0.5
    for h in range(N_HEADS):
        cols = slice(h * HEAD_DIM, (h + 1) * HEAD_DIM)
        head_rows = pl.ds(h, n_cache, stride=N_HEADS)
        q = q_ref[:, cols]
        s_c = _dot_nt(q, kc_ref[0, head_rows, :].astype(BF16)) * scale + bias_ref[h, :, :n_cache]
        s_n = _dot_nt(q, kn_ref[:, cols]) * scale + bias_ref[h, :, n_cache:]
        mx = jnp.maximum(s_c.max(-1, keepdims=True), s_n.max(-1, keepdims=True))
        e_c, e_n = jnp.exp(s_c - mx), jnp.exp(s_n - mx)
        den = e_c.sum(-1, keepdims=True) + e_n.sum(-1, keepdims=True)
        o = (_dot(e_c.astype(BF16), vc_ref[0, head_rows, :].astype(BF16))
             + _dot(e_n.astype(BF16), vn_ref[:, cols]))
        o_ref[:, cols] = (o / den).astype(BF16)


def _sample_bias(rel_bias, n_cache, n_new):
    q_pos = PAST_LEN + np.arange(n_new)
    k_pos = np.concatenate([PAST_LEN - n_cache + np.arange(n_cache), PAST_LEN + np.arange(n_new)])
    cq, ck = q_pos[:, None] // CHUNK, k_pos[None, :] // CHUNK
    allowed = (ck <= cq) & (cq - ck <= N_LEFT_CHUNKS)
    bias = jnp.concatenate([_rel_bias_table(rel_bias, n_new, n_cache, n_cache),
                            _rel_bias_table(rel_bias, n_new, n_new, 0)], axis=2)
    return jnp.where(allowed[None], bias, NEG_INF)


def _attn_sample(hact, cache_k, cache_v, bias, *, n_streams, n_new):
    n_cache = cache_k.shape[1] // N_HEADS
    new_spec = lambda col: pl.BlockSpec((n_new, COL), lambda b: (b, col))
    cache_spec = pl.BlockSpec((1, n_cache * N_HEADS, HEAD_DIM), lambda b: (b, 0, 0))
    return pl.pallas_call(
        _attn_sample_kernel,
        grid=(n_streams,),
        in_specs=[new_spec(COL_Q), new_spec(COL_K), new_spec(COL_V), cache_spec, cache_spec,
                  pl.BlockSpec((N_HEADS, n_new, n_cache + n_new), lambda b: (0, 0, 0))],
        out_specs=pl.BlockSpec((n_new, D_ATTN), lambda b: (b, 0)),
        out_shape=jax.ShapeDtypeStruct((n_streams * n_new, D_ATTN), BF16),
        compiler_params=pltpu.CompilerParams(
            dimension_semantics=("arbitrary",), vmem_limit_bytes=VMEM_LIMIT),
        name="attn_sample",
    )(hact, hact, hact, cache_k, cache_v, bias)


def _merge_kernel(x_ref, u_ref, vb_ref, ga0_ref, ga1_ref, gb0_ref, gb1_ref, a_ref,
                  wbd_ref, sb_ref, wa_ref, wb_ref, wo_ref, o_ref, s_ref):
    tm = x_ref.shape[0]
    for c in range(tm // SGU_SUB):
        rows = slice(c * SGU_SUB, (c + 1) * SGU_SUB)
        for g in range(N_GROUPS):
            cols = slice(g * GROUP_DIM, (g + 1) * GROUP_DIM)
            mixed = _dot(wbd_ref[g], vb_ref[rows, cols]) + sb_ref[:, cols]
            s_ref[rows, cols] = (u_ref[rows, cols].astype(F32) * mixed).astype(BF16)
    pa = _dot(a_ref[...], wa_ref[...])
    pb = _dot(s_ref[...], wb_ref[...])
    half = D_MODEL // 2
    m0 = ga0_ref[...].astype(F32) * pa[:, :half] + gb0_ref[...].astype(F32) * pb[:, :half]
    m1 = ga1_ref[...].astype(F32) * pa[:, half:] + gb1_ref[...].astype(F32) * pb[:, half:]
    m = jnp.concatenate([m0, m1], axis=-1).astype(BF16)
    o_ref[...] = x_ref[...] + _dot(m, wo_ref[...])


def _sgu_block_weights(w_s, b_s, chunk):
    reps = SGU_SUB // chunk
    w = (w_s * jnp.tril(jnp.ones((SGU_CHUNK, SGU_CHUNK), w_s.dtype)))[:, :chunk, :chunk]
    eye = jnp.eye(reps, dtype=w.dtype)
    wbd = (eye[None, :, None, :, None] * w[:, None, :, None, :]).reshape(N_GROUPS, SGU_SUB, SGU_SUB)
    bias = jnp.repeat(jnp.tile(b_s[:, :chunk].T, (reps, 1)), GROUP_DIM, axis=1)
    return wbd.astype(BF16), bias.astype(F32)


def _merge(x, hact, a, wbd, sbias, wa, wb, wo, *, tm):
    m = x.shape[0]
    hcol = lambda col: pl.BlockSpec((tm, COL), lambda i: (i, col))
    const = lambda shape: pl.BlockSpec(shape, lambda i: (0,) * len(shape), pipeline_mode=pl.Buffered(1))
    return pl.pallas_call(
        _merge_kernel,
        grid=(m // tm,),
        in_specs=[pl.BlockSpec((tm, D_MODEL), lambda i: (i, 0)),
                  hcol(COL_U), hcol(COL_VB), hcol(COL_GA), hcol(COL_GA + 1), hcol(COL_GB), hcol(COL_GB + 1),
                  pl.BlockSpec((tm, D_ATTN), lambda i: (i, 0)),
                  const((N_GROUPS, SGU_SUB, SGU_SUB)), const((SGU_SUB, D_SGU)),
                  const((D_ATTN, D_MODEL)), const((D_SGU, D_MODEL)), const((D_MODEL, D_MODEL))],
        out_specs=pl.BlockSpec((tm, D_MODEL), lambda i: (i, 0)),
        out_shape=jax.ShapeDtypeStruct((m, D_MODEL), F32),
        scratch_shapes=[pltpu.VMEM((tm, D_SGU), BF16)],
        compiler_params=pltpu.CompilerParams(
            dimension_semantics=("arbitrary",), vmem_limit_bytes=VMEM_LIMIT),
        name="merge",
    )(x, hact, hact, hact, hact, hact, hact, a, wbd, sbias, wa, wb, wo)


def _conv_gate(hg, hv, prev_g, prev_v, cwg, cwv, cbg, cbv):
    def conv(h, prev, cw, cb):
        return cb + cw[0:1, :] * prev(2) + cw[1:2, :] * prev(1) + cw[2:3, :] * h
    return (_gelu(conv(hg, prev_g, cwg, cbg)) * conv(hv, prev_v, cwv, cbv)).astype(BF16)


def _ffn_prologue(j, x_ref, g_ref, y_ref, xn_ref):
    @pl.when(j == 0)
    def _():
        x = x_ref[...]
        xn_ref[...] = _rms(x, g_ref[...]).astype(BF16)
        y_ref[...] = x


def _ffn_epilogue(j, gf_ref, y_ref):
    @pl.when(j == N_FF_TILES - 1)
    def _():
        y_ref[...] = _rms(y_ref[...], gf_ref[...])


def _ffn_prompt_kernel(x_ref, g_ref, wg_ref, wv_ref, cw_ref, cb_ref, wd_ref, gf_ref,
                       y_ref, hl_ref, xn_ref, cg_ref, cv_ref, *, blocks_per_seq):
    i = pl.program_id(0)
    j = pl.program_id(1)
    tm = x_ref.shape[0]

    @pl.when(i % blocks_per_seq == 0)
    def _():
        cg_ref[j] = jnp.zeros(cg_ref.shape[1:], F32)
        cv_ref[j] = jnp.zeros(cv_ref.shape[1:], F32)

    def delayed(h, carry):
        def prev(k):
            head = jnp.concatenate([carry, h[:CARRY_ROWS]], axis=0)[CARRY_ROWS - k:2 * CARRY_ROWS - k]
            return jnp.concatenate([head, pltpu.roll(h, k, axis=0)[CARRY_ROWS:]], axis=0)
        return prev

    def step(first, last):
        if first:
            xn_ref[...] = _rms(x_ref[...], g_ref[...]).astype(BF16)
        hg = _dot(xn_ref[...], wg_ref[...])
        hv = _dot(xn_ref[...], wv_ref[...])
        act = _conv_gate(hg, hv, delayed(hg, cg_ref[j]), delayed(hv, cv_ref[j]),
                         cw_ref[j], cw_ref[N_FF_TILES + j], cb_ref[j], cb_ref[N_FF_TILES + j])
        y = (x_ref if first else y_ref)[...] + _dot(act, wd_ref[...])
        y_ref[...] = _rms(y, gf_ref[...]) if last else y
        last_g, last_v = hg[tm - CARRY_ROWS:], hv[tm - CARRY_ROWS:]
        cg_ref[j] = last_g
        cv_ref[j] = last_v
        hl_ref[0, 0, j] = last_g
        hl_ref[0, 1, j] = last_v

    @pl.when(j == 0)
    def _():
        step(True, False)

    @pl.when((j > 0) & (j < N_FF_TILES - 1))
    def _():
        step(False, False)

    @pl.when(j == N_FF_TILES - 1)
    def _():
        step(False, True)


def _ffn_sample_kernel(x_ref, g_ref, wg_ref, wv_ref, cwg_ref, cwv_ref, cbg_ref, cbv_ref, wd_ref, gf_ref,
                       pg_ref, pv_ref, y_ref, hg_ref, hv_ref, xn_ref, *, seq):
    j = pl.program_id(1)
    tm = x_ref.shape[0]
    _ffn_prologue(j, x_ref, g_ref, y_ref, xn_ref)
    hg = _dot(xn_ref[...], wg_ref[...])
    hv = _dot(xn_ref[...], wv_ref[...])
    hg_ref[...] = hg
    hv_ref[...] = hv
    pos = lax.broadcasted_iota(jnp.int32, (tm, FF_TILE), 0) % seq

    def delayed(h, hist_ref):
        def expand(t):
            n = hist_ref.shape[0]
            return jnp.broadcast_to(hist_ref[:, t:t + 1, :], (n, seq, FF_TILE)).reshape(tm, FF_TILE)

        def prev(k):
            rolled = pltpu.roll(h, k, axis=0)
            if k == 1:
                return jnp.where(pos == 0, expand(1), rolled)
            return jnp.where(pos == 0, expand(0), jnp.where(pos == 1, expand(1), rolled))
        return prev

    act = _conv_gate(hg, hv, delayed(hg, pg_ref), delayed(hv, pv_ref),
                     cwg_ref[...], cwv_ref[...], cbg_ref[...], cbv_ref[...])
    y_ref[...] += _dot(act, wd_ref[...])
    _ffn_epilogue(j, gf_ref, y_ref)


def _ffn_common_specs(tm):
    row = lambda i, j: (i, 0)
    fixed = lambda i, j: (0, 0)
    gate = lambda i, j: (0, j)
    val = lambda i, j: (0, N_FF_TILES + j)
    return [
        pl.BlockSpec((tm, D_MODEL), row),
        pl.BlockSpec((1, D_MODEL), fixed),
        pl.BlockSpec((D_MODEL, FF_TILE), gate),
        pl.BlockSpec((D_MODEL, FF_TILE), val),
        pl.BlockSpec((CONV_W, FF_TILE), gate),
        pl.BlockSpec((CONV_W, FF_TILE), val),
        pl.BlockSpec((1, FF_TILE), gate),
        pl.BlockSpec((1, FF_TILE), val),
        pl.BlockSpec((FF_TILE, D_MODEL), lambda i, j: (j, 0)),
        pl.BlockSpec((1, D_MODEL), fixed),
    ]


def _ffn_prompt(x, g, w_up, conv_w, conv_b, w_down, gf, *, tm, rows_per_seq):
    m = x.shape[0]
    blocks_per_seq = rows_per_seq // tm
    n_tiles = 2 * N_FF_TILES
    cw_tiles = conv_w.reshape(CONV_W, n_tiles, FF_TILE).transpose(1, 0, 2)
    cb_tiles = conv_b.reshape(n_tiles, 1, FF_TILE)
    row = lambda i, j: (i, 0)
    fixed2 = lambda i, j: (0, 0)
    fixed3 = lambda i, j: (0, 0, 0)
    last_shape = (1, 2, N_FF_TILES, CARRY_ROWS, FF_TILE)
    carry = pltpu.VMEM((N_FF_TILES, CARRY_ROWS, FF_TILE), F32)
    y, h_last = pl.pallas_call(
        functools.partial(_ffn_prompt_kernel, blocks_per_seq=blocks_per_seq),
        grid=(m // tm, N_FF_TILES),
        in_specs=[
            pl.BlockSpec((tm, D_MODEL), row),
            pl.BlockSpec((1, D_MODEL), fixed2),
            pl.BlockSpec((D_MODEL, FF_TILE), lambda i, j: (0, j)),
            pl.BlockSpec((D_MODEL, FF_TILE), lambda i, j: (0, N_FF_TILES + j)),
            pl.BlockSpec((n_tiles, CONV_W, FF_TILE), fixed3),
            pl.BlockSpec((n_tiles, 1, FF_TILE), fixed3),
            pl.BlockSpec((FF_TILE, D_MODEL), lambda i, j: (j, 0)),
            pl.BlockSpec((1, D_MODEL), fixed2),
        ],
        out_specs=[pl.BlockSpec((tm, D_MODEL), row), pl.BlockSpec(last_shape, lambda i, j: (i, 0, 0, 0, 0))],
        out_shape=[jax.ShapeDtypeStruct((m, D_MODEL), F32),
                   jax.ShapeDtypeStruct((m // tm,) + last_shape[1:], F32)],
        scratch_shapes=[pltpu.VMEM((tm, D_MODEL), BF16), carry, carry],
        compiler_params=pltpu.CompilerParams(
            dimension_semantics=("arbitrary", "arbitrary"), vmem_limit_bytes=VMEM_LIMIT_FFN),
        name="ffn_prompt",
    )(x, g, w_up, w_up, cw_tiles, cb_tiles, w_down, gf)
    tail = h_last[blocks_per_seq - 1::blocks_per_seq, :, :, CARRY_ROWS - (CONV_W - 1):, :]
    return y, tail.transpose(0, 3, 1, 2, 4).reshape(tail.shape[0], CONV_W - 1, 2 * D_FF)


def _ffn_sample(x, g, w_up, conv_w, conv_b, w_down, gf, hist, *, seq):
    m = x.shape[0]
    n_streams = m // seq
    hist_g = pl.BlockSpec((n_streams, CONV_W - 1, FF_TILE), lambda i, j: (0, 0, j))
    hist_v = pl.BlockSpec((n_streams, CONV_W - 1, FF_TILE), lambda i, j: (0, 0, N_FF_TILES + j))
    h_spec = pl.BlockSpec((m, FF_TILE), lambda i, j: (0, j))
    h_shape = jax.ShapeDtypeStruct((m, D_FF), F32)
    return pl.pallas_call(
        functools.partial(_ffn_sample_kernel, seq=seq),
        grid=(1, N_FF_TILES),
        in_specs=_ffn_common_specs(m) + [hist_g, hist_v],
        out_specs=[pl.BlockSpec((m, D_MODEL), lambda i, j: (0, 0)), h_spec, h_spec],
        out_shape=[jax.ShapeDtypeStruct((m, D_MODEL), F32), h_shape, h_shape],
        scratch_shapes=[pltpu.VMEM((m, D_MODEL), BF16)],
        compiler_params=pltpu.CompilerParams(
            dimension_semantics=("arbitrary", "arbitrary"), vmem_limit_bytes=VMEM_LIMIT),
        name="ffn_sample",
    )(x, g, w_up, w_up, conv_w, conv_w, conv_b, conv_b, w_down, gf, hist, hist)


def kernel(x_prompt, x_sample, cache_k, cache_v, cache_ffn_conv, norm_mix_g, w_in, rel_bias, sgu_norm_g, w_s, b_s,
           w_branch_a, w_branch_b, w_out, norm_ffn_g, w_up, conv_w, conv_b, w_down, norm_final_g):
    depth = w_in.shape[0]
    assert depth == 1, "single-layer trunk"
    batch, seq, _ = x_prompt.shape
    n_streams, n_new, _ = x_sample.shape
    n_cache = cache_k.shape[2]
    keep = min(KV_WINDOW, seq)

    row = lambda v: v.reshape(1, -1).astype(F32)
    w_in_bf = w_in[0].astype(BF16)
    g_mix, g_sgu, g_ffn, g_fin = row(norm_mix_g[0]), row(sgu_norm_g[0]), row(norm_ffn_g[0]), row(norm_final_g)
    cw, cb = conv_w[0].astype(F32), row(conv_b[0])

    xp = x_prompt.reshape(batch * seq, D_MODEL)
    hact_p, k_tail, v_tail, wa_bf, wb_bf, wo_bf, w_up_bf, w_down_bf = _in_proj(
        xp, g_mix, w_in_bf, g_sgu, tm=min(PROJ_ROWS, seq), rows_per_seq=seq, tail=keep, vb_tail=False,
        cast=(w_branch_a[0], w_branch_b[0], w_out[0], w_up[0], w_down[0]))
    a_p = _attn_prompt(hact_p, _prompt_bias(rel_bias[0]), batch=batch, seq=seq)
    wbd_p, sb_p = _sgu_block_weights(w_s[0], b_s[0], SGU_CHUNK)
    x1_p = _merge(xp, hact_p, a_p, wbd_p, sb_p, wa_bf, wb_bf, wo_bf, tm=MERGE_ROWS)
    ffn_rows = min(FFN_ROWS, seq)
    y_p, conv_p = _ffn_prompt(x1_p, g_ffn, w_up_bf, cw, cb, w_down_bf, g_fin, tm=ffn_rows, rows_per_seq=seq)

    ms = n_streams * n_new
    xs = x_sample.reshape(ms, D_MODEL)
    hact_s, k_new, v_new, vb_new = _in_proj(xs, g_mix, w_in_bf, g_sgu, tm=ms, rows_per_seq=ms, tail=ms,
                                            vb_tail=True)
    a_s = _attn_sample(hact_s, cache_k[0].reshape(n_streams, n_cache * N_HEADS, HEAD_DIM),
                       cache_v[0].reshape(n_streams, n_cache * N_HEADS, HEAD_DIM),
                       _sample_bias(rel_bias[0], n_cache, n_new), n_streams=n_streams, n_new=n_new)
    wbd_s, sb_s = _sgu_block_weights(w_s[0], b_s[0], n_new)
    x1_s = _merge(xs, hact_s, a_s, wbd_s, sb_s, wa_bf, wb_bf, wo_bf, tm=ms)
    y_s, h_g, h_v = _ffn_sample(x1_s, g_ffn, w_up_bf, cw, cb, w_down_bf, g_fin,
                                cache_ffn_conv[0], seq=n_new)

    hist = CONV_W - 1
    heads = lambda t, b, s: t.reshape(1, b, s, N_HEADS, HEAD_DIM)
    new_conv_prompt = conv_p[None]
    h_s = jnp.concatenate([h_g, h_v], axis=-1).reshape(n_streams, n_new, 2 * D_FF)
    new_conv_sample = h_s[:, n_new - hist:][None]
    return (
        y_p.reshape(batch, seq, D_MODEL),
        y_s.reshape(n_streams, n_new, D_MODEL),
        heads(k_tail, batch, keep),
        heads(v_tail, batch, keep),
        heads(k_new, n_streams, n_new),
        heads(v_new, n_streams, n_new),
        vb_new.reshape(1, n_streams, n_new, D_SGU),
        new_conv_prompt,
        new_conv_sample,
    )
```

```python
import functools

import numpy as np
import jax
import jax.numpy as jnp
from jax import lax
from jax.experimental import pallas as pl
from jax.experimental.pallas import tpu as pltpu

D_MODEL = 2048
CHUNK = 64
N_LEFT_CHUNKS = 8
KV_WINDOW = N_LEFT_CHUNKS * CHUNK
D_ATTN = D_MODEL // 2
N_HEADS = 8
HEAD_DIM = D_ATTN // N_HEADS
MAX_REL = 256
D_SGU = D_MODEL // 2
N_GROUPS = 8
GROUP_DIM = D_SGU // N_GROUPS
SGU_CHUNK = 128
D_FF = 5632
CONV_W = 3
EPS = 1e-6
PAST_LEN = 2048
D_IN = 3 * D_ATTN + 2 * D_SGU + 2 * D_MODEL
NEG_INF = -1e30

COL = 1024
COL_Q, COL_K, COL_V, COL_U, COL_VB, COL_GA, COL_GB = 0, 1, 2, 3, 4, 5, 7
N_COL_BLOCKS = D_IN // COL

PROJ_ROWS = 1024
MERGE_ROWS = 512
FFN_ROWS = 1024
FFN_SUB = 256
PROJ_COL_CHUNK = 512
PROJ_ROW_CHUNK = 256
ATTN_QB = 256
LOG2E = float(np.log2(np.e))
SGU_SUB = 256
FF_TILE = 512
N_FF_TILES = D_FF // FF_TILE
CARRY_ROWS = 8
BF16_SUBLANES = 16

VMEM_LIMIT = 56 * 1024 * 1024
VMEM_LIMIT_FFN = 60 * 1024 * 1024

BF16 = jnp.bfloat16
F32 = jnp.float32


def _rms(x, g):
    inv = lax.rsqrt(jnp.mean(x * x, axis=-1, keepdims=True) + EPS)
    return (x * inv) * g


def _gelu(x):
    return 0.5 * x * (1.0 + lax.erf(x * (2.0 ** -0.5)))


def _sigmoid(x):
    return 0.5 * jnp.tanh(0.5 * x) + 0.5


def _dot(a, b):
    return jnp.dot(a, b, preferred_element_type=F32)


def _dot_nt(a, b):
    return lax.dot_general(a, b, (((1,), (1,)), ((), ())), preferred_element_type=F32)


def _in_proj_kernel(x_ref, g_ref, w_ref, sg_ref, *refs, tail, vb_tail, n_cast):
    cast_in, refs = refs[:n_cast], refs[n_cast:]
    h_ref, kt_ref, vt_ref = refs[:3]
    vbt_ref = refs[3] if vb_tail else None
    cast_out, xn_ref = refs[3 + vb_tail:-1], refs[-1]
    j = pl.program_id(1)
    tm = x_ref.shape[0]
    head = tm - tail

    @pl.when(j == 0)
    def _():
        xn_ref[...] = _rms(x_ref[...], g_ref[...]).astype(BF16)

    def side_cast():
        for src, dst in zip(cast_in, cast_out):
            dst[...] = src[...].astype(BF16)

    def by_cols(act, tail_ref=None):
        side_cast()
        for c in range(COL // PROJ_COL_CHUNK):
            cols = slice(c * PROJ_COL_CHUNK, (c + 1) * PROJ_COL_CHUNK)
            acc = _dot(xn_ref[...], w_ref[:, cols])
            h_ref[:, cols] = act(acc).astype(BF16)
            if tail_ref is not None:
                tail_ref[:, cols] = acc[head:, :]

    @pl.when(j == COL_Q)
    def _():
        by_cols(lambda a: a)

    @pl.when(j == COL_K)
    def _():
        by_cols(lambda a: a, kt_ref)

    @pl.when(j == COL_V)
    def _():
        by_cols(lambda a: a, vt_ref)

    @pl.when(j == COL_U)
    def _():
        by_cols(_gelu)

    @pl.when(j == COL_VB)
    def _():
        side_cast()
        for r in range(tm // PROJ_ROW_CHUNK):
            lo = r * PROJ_ROW_CHUNK
            rows = slice(lo, lo + PROJ_ROW_CHUNK)
            vb = _rms(_gelu(_dot(xn_ref[rows, :], w_ref[...])), sg_ref[...])
            h_ref[rows, :] = vb.astype(BF16)
            if vbt_ref is not None and lo >= head:
                vbt_ref[lo - head:lo - head + PROJ_ROW_CHUNK, :] = vb

    @pl.when(j >= COL_GA)
    def _():
        by_cols(_sigmoid)


def _in_proj(x, g, w_bf, sg, *, tm, rows_per_seq, tail, vb_tail, cast=()):
    m = x.shape[0]
    blocks_per_seq = rows_per_seq // tm
    n_seq = m // rows_per_seq
    assert tail <= tm and (tm - tail) % PROJ_ROW_CHUNK == 0
    n_tails = 3 if vb_tail else 2
    n_steps = (m // tm) * N_COL_BLOCKS
    tail_spec = pl.BlockSpec((tail, COL), lambda i, j: (i // blocks_per_seq, 0))
    tail_shape = jax.ShapeDtypeStruct((n_seq * tail, COL), F32)

    def slab_spec(w):
        rows = BF16_SUBLANES
        while w.shape[0] // rows > n_steps:
            rows *= 2
        assert w.shape[0] % rows == 0
        last = w.shape[0] // rows - 1
        return pl.BlockSpec((rows, w.shape[1]), lambda i, j: (jnp.minimum(i * N_COL_BLOCKS + j, last), 0))

    slab_specs = [slab_spec(w) for w in cast]
    return pl.pallas_call(
        functools.partial(_in_proj_kernel, tail=tail, vb_tail=vb_tail, n_cast=len(cast)),
        grid=(m // tm, N_COL_BLOCKS),
        in_specs=[
            pl.BlockSpec((tm, D_MODEL), lambda i, j: (i, 0)),
            pl.BlockSpec((1, D_MODEL), lambda i, j: (0, 0)),
            pl.BlockSpec((D_MODEL, COL), lambda i, j: (0, j)),
            pl.BlockSpec((1, D_SGU), lambda i, j: (0, 0)),
        ] + slab_specs,
        out_specs=[pl.BlockSpec((tm, COL), lambda i, j: (i, j))] + [tail_spec] * n_tails + slab_specs,
        out_shape=[jax.ShapeDtypeStruct((m, D_IN), BF16)] + [tail_shape] * n_tails
        + [jax.ShapeDtypeStruct(w.shape, BF16) for w in cast],
        scratch_shapes=[pltpu.VMEM((tm, D_MODEL), BF16)],
        compiler_params=pltpu.CompilerParams(
            dimension_semantics=("arbitrary", "arbitrary"), vmem_limit_bytes=VMEM_LIMIT),
        name="in_proj",
    )(x, g, w_bf, sg, *cast)


def _attn_prompt_kernel(q_ref, k0_ref, k1_ref, k2_ref, v0_ref, v1_ref, v2_ref, bias_ref, o_ref):
    qb = pl.program_id(1)
    k_refs = (k0_ref, k1_ref, k2_ref)
    v_refs = (v0_ref, v1_ref, v2_ref)
    scale2 = HEAD_DIM ** -0.5 * LOG2E

    ones = jnp.ones((3 * ATTN_QB, HEAD_DIM), BF16)

    def attend(mask_missing):
        for h in range(N_HEADS):
            cols = slice(h * HEAD_DIM, (h + 1) * HEAD_DIM)
            k = jnp.concatenate([r[:, cols] for r in k_refs], axis=0)
            v = jnp.concatenate([jnp.concatenate([r[:, cols] for r in v_refs], axis=0), ones], axis=1)
            s = _dot_nt(q_ref[:, cols], k) * scale2 + bias_ref[h]
            if mask_missing:
                s = jnp.concatenate(
                    [s[:, r * ATTN_QB:(r + 1) * ATTN_QB] + jnp.where(qb - 2 + r >= 0, 0.0, NEG_INF).astype(F32)
                     for r in range(3)], axis=1)
            e = jnp.exp2(s - s.max(-1, keepdims=True)).astype(BF16)
            o = _dot(e, v)
            o_ref[:, cols] = (o[:, :HEAD_DIM] / o[:, HEAD_DIM:]).astype(BF16)

    @pl.when(qb < KV_WINDOW // ATTN_QB)
    def _():
        attend(True)

    @pl.when(qb >= KV_WINDOW // ATTN_QB)
    def _():
        attend(False)


def _rel_bias_table(rel_bias, n, m, d0):
    length = n + m - 1
    dist = np.arange(length) - (m - 1) + d0
    diag = rel_bias[:, np.clip(dist, -MAX_REL, MAX_REL) + MAX_REL].astype(F32)
    rev = diag[:, ::-1]
    padded = jnp.concatenate([rev, rev[:, :1]], axis=1)
    skew = jnp.tile(padded, (1, n))[:, :n * length].reshape(-1, n, length)
    return skew[:, :, n - 1:n - 1 + m]


def _prompt_bias(rel_bias):
    qi = np.arange(ATTN_QB)[:, None]
    kp = np.arange(3 * ATTN_QB)[None, :] - 2 * ATTN_QB
    cq, ck = qi // CHUNK, np.floor_divide(kp, CHUNK)
    allowed = (ck <= cq) & (cq - ck <= N_LEFT_CHUNKS)
    bias = _rel_bias_table(rel_bias, ATTN_QB, 3 * ATTN_QB, 2 * ATTN_QB) * LOG2E
    return jnp.where(allowed[None], bias, NEG_INF)


def _attn_prompt(hact, bias, *, batch, seq):
    nqb = seq // ATTN_QB
    kv_spec = lambda col, r: pl.BlockSpec(
        (ATTN_QB, COL), lambda b, t: (b * nqb + jnp.maximum(t - 2 + r, 0), col))
    return pl.pallas_call(
        _attn_prompt_kernel,
        grid=(batch, nqb),
        in_specs=[pl.BlockSpec((ATTN_QB, COL), lambda b, t: (b * nqb + t, COL_Q))]
        + [kv_spec(COL_K, r) for r in range(3)] + [kv_spec(COL_V, r) for r in range(3)]
        + [pl.BlockSpec((N_HEADS, ATTN_QB, 3 * ATTN_QB), lambda b, t: (0, 0, 0))],
        out_specs=pl.BlockSpec((ATTN_QB, D_ATTN), lambda b, t: (b * nqb + t, 0)),
        out_shape=jax.ShapeDtypeStruct((batch * seq, D_ATTN), BF16),
        compiler_params=pltpu.CompilerParams(
            dimension_semantics=("arbitrary", "arbitrary"), vmem_limit_bytes=VMEM_LIMIT),
        name="attn_prompt",
    )(hact, hact, hact, hact, hact, hact, hact, bias)


def _attn_sample_kernel(q_ref, kn_ref, vn_ref, kc_ref, vc_ref, bias_ref, o_ref):
    n_cache = kc_ref.shape[1] // N_HEADS
    scale = HEAD_DIM ** -0.5
    for h in range(N_HEADS):
        cols = slice(h * HEAD_DIM, (h + 1) * HEAD_DIM)
        head_rows = pl.ds(h, n_cache, stride=N_HEADS)
        q = q_ref[:, cols]
        s_c = _dot_nt(q, kc_ref[0, head_rows, :].astype(BF16)) * scale + bias_ref[h, :, :n_cache]
        s_n = _dot_nt(q, kn_ref[:, cols]) * scale + bias_ref[h, :, n_cache:]
        mx = jnp.maximum(s_c.max(-1, keepdims=True), s_n.max(-1, keepdims=True))
        e_c, e_n = jnp.exp(s_c - mx), jnp.exp(s_n - mx)
        den = e_c.sum(-1, keepdims=True) + e_n.sum(-1, keepdims=True)
        o = (_dot(e_c.astype(BF16), vc_ref[0, head_rows, :].astype(BF16))
             + _dot(e_n.astype(BF16), vn_ref[:, cols]))
        o_ref[:, cols] = (o / den).astype(BF16)


def _sample_bias(rel_bias, n_cache, n_new):
    q_pos = PAST_LEN + np.arange(n_new)
    k_pos = np.concatenate([PAST_LEN - n_cache + np.arange(n_cache), PAST_LEN + np.arange(n_new)])
    cq, ck = q_pos[:, None] // CHUNK, k_pos[None, :] // CHUNK
    allowed = (ck <= cq) & (cq - ck <= N_LEFT_CHUNKS)
    bias = jnp.concatenate([_rel_bias_table(rel_bias, n_new, n_cache, n_cache),
                            _rel_bias_table(rel_bias, n_new, n_new, 0)], axis=2)
    return jnp.where(allowed[None], bias, NEG_INF)


def _attn_sample(hact, cache_k, cache_v, bias, *, n_streams, n_new):
    n_cache = cache_k.shape[1] // N_HEADS
    new_spec = lambda col: pl.BlockSpec((n_new, COL), lambda b: (b, col))
    cache_spec = pl.BlockSpec((1, n_cache * N_HEADS, HEAD_DIM), lambda b: (b, 0, 0))
    return pl.pallas_call(
        _attn_sample_kernel,
        grid=(n_streams,),
        in_specs=[new_spec(COL_Q), new_spec(COL_K), new_spec(COL_V), cache_spec, cache_spec,
                  pl.BlockSpec((N_HEADS, n_new, n_cache + n_new), lambda b: (0, 0, 0))],
        out_specs=pl.BlockSpec((n_new, D_ATTN), lambda b: (b, 0)),
        out_shape=jax.ShapeDtypeStruct((n_streams * n_new, D_ATTN), BF16),
        compiler_params=pltpu.CompilerParams(
            dimension_semantics=("arbitrary",), vmem_limit_bytes=VMEM_LIMIT),
        name="attn_sample",
    )(hact, hact, hact, cache_k, cache_v, bias)


def _merge_kernel(x_ref, u_ref, vb_ref, ga0_ref, ga1_ref, gb0_ref, gb1_ref, a_ref,
                  wbd_ref, sb_ref, wa_ref, wb_ref, wo_ref, o_ref, s_ref):
    tm = x_ref.shape[0]
    for c in range(tm // SGU_SUB):
        rows = slice(c * SGU_SUB, (c + 1) * SGU_SUB)
        for g in range(N_GROUPS):
            cols = slice(g * GROUP_DIM, (g + 1) * GROUP_DIM)
            mixed = _dot(wbd_ref[g], vb_ref[rows, cols]) + sb_ref[:, cols]
            s_ref[rows, cols] = (u_ref[rows, cols].astype(F32) * mixed).astype(BF16)
    pa = _dot(a_ref[...], wa_ref[...])
    pb = _dot(s_ref[...], wb_ref[...])
    half = D_MODEL // 2
    m0 = ga0_ref[...].astype(F32) * pa[:, :half] + gb0_ref[...].astype(F32) * pb[:, :half]
    m1 = ga1_ref[...].astype(F32) * pa[:, half:] + gb1_ref[...].astype(F32) * pb[:, half:]
    m = jnp.concatenate([m0, m1], axis=-1).astype(BF16)
    o_ref[...] = x_ref[...] + _dot(m, wo_ref[...])


def _sgu_block_weights(w_s, b_s, chunk):
    reps = SGU_SUB // chunk
    w = (w_s * jnp.tril(jnp.ones((SGU_CHUNK, SGU_CHUNK), w_s.dtype)))[:, :chunk, :chunk]
    eye = jnp.eye(reps, dtype=w.dtype)
    wbd = (eye[None, :, None, :, None] * w[:, None, :, None, :]).reshape(N_GROUPS, SGU_SUB, SGU_SUB)
    bias = jnp.repeat(jnp.tile(b_s[:, :chunk].T, (reps, 1)), GROUP_DIM, axis=1)
    return wbd.astype(BF16), bias.astype(F32)


def _merge(x, hact, a, wbd, sbias, wa, wb, wo, *, tm):
    m = x.shape[0]
    hcol = lambda col: pl.BlockSpec((tm, COL), lambda i: (i, col))
    const = lambda shape: pl.BlockSpec(shape, lambda i: (0,) * len(shape), pipeline_mode=pl.Buffered(1))
    return pl.pallas_call(
        _merge_kernel,
        grid=(m // tm,),
        in_specs=[pl.BlockSpec((tm, D_MODEL), lambda i: (i, 0)),
                  hcol(COL_U), hcol(COL_VB), hcol(COL_GA), hcol(COL_GA + 1), hcol(COL_GB), hcol(COL_GB + 1),
                  pl.BlockSpec((tm, D_ATTN), lambda i: (i, 0)),
                  const((N_GROUPS, SGU_SUB, SGU_SUB)), const((SGU_SUB, D_SGU)),
                  const((D_ATTN, D_MODEL)), const((D_SGU, D_MODEL)), const((D_MODEL, D_MODEL))],
        out_specs=pl.BlockSpec((tm, D_MODEL), lambda i: (i, 0)),
        out_shape=jax.ShapeDtypeStruct((m, D_MODEL), F32),
        scratch_shapes=[pltpu.VMEM((tm, D_SGU), BF16)],
        compiler_params=pltpu.CompilerParams(
            dimension_semantics=("arbitrary",), vmem_limit_bytes=VMEM_LIMIT),
        name="merge",
    )(x, hact, hact, hact, hact, hact, hact, a, wbd, sbias, wa, wb, wo)


def _conv_gate(hg, hv, prev_g, prev_v, cwg, cwv, cbg, cbv):
    def conv(h, prev, cw, cb):
        return cb + cw[0:1, :] * prev(2) + cw[1:2, :] * prev(1) + cw[2:3, :] * h
    return (_gelu(conv(hg, prev_g, cwg, cbg)) * conv(hv, prev_v, cwv, cbv)).astype(BF16)


def _ffn_prologue(j, x_ref, g_ref, y_ref, xn_ref):
    @pl.when(j == 0)
    def _():
        x = x_ref[...]
        xn_ref[...] = _rms(x, g_ref[...]).astype(BF16)
        y_ref[...] = x


def _ffn_epilogue(j, gf_ref, y_ref):
    @pl.when(j == N_FF_TILES - 1)
    def _():
        y_ref[...] = _rms(y_ref[...], gf_ref[...])


def _ffn_prompt_kernel(x_ref, g_ref, wg_ref, wv_ref, cw_ref, cb_ref, wd_ref, gf_ref,
                       y_ref, hl_ref, xn_ref, cg_ref, cv_ref, *, blocks_per_seq):
    i = pl.program_id(0)
    j = pl.program_id(1)
    tm = x_ref.shape[0]

    @pl.when(i % blocks_per_seq == 0)
    def _():
        cg_ref[j] = jnp.zeros(cg_ref.shape[1:], F32)
        cv_ref[j] = jnp.zeros(cv_ref.shape[1:], F32)

    def delayed(h, carry):
        def prev(k):
            head = jnp.concatenate([carry, h[:CARRY_ROWS]], axis=0)[CARRY_ROWS - k:2 * CARRY_ROWS - k]
            return jnp.concatenate([head, pltpu.roll(h, k, axis=0)[CARRY_ROWS:]], axis=0)
        return prev

    def step(first, last):
        if first:
            xn_ref[...] = _rms(x_ref[...], g_ref[...]).astype(BF16)
        up = lambda w_ref: jnp.concatenate(
            [_dot(xn_ref[r:r + FFN_SUB, :], w_ref[...]) for r in range(0, tm, FFN_SUB)], axis=0)
        hg = up(wg_ref)
        hv = up(wv_ref)
        act = _conv_gate(hg, hv, delayed(hg, cg_ref[j]), delayed(hv, cv_ref[j]),
                         cw_ref[j], cw_ref[N_FF_TILES + j], cb_ref[j], cb_ref[N_FF_TILES + j])
        y = (x_ref if first else y_ref)[...] + _dot(act, wd_ref[...])
        y_ref[...] = _rms(y, gf_ref[...]) if last else y
        last_g, last_v = hg[tm - CARRY_ROWS:], hv[tm - CARRY_ROWS:]
        cg_ref[j] = last_g
        cv_ref[j] = last_v
        hl_ref[0, 0, j] = last_g
        hl_ref[0, 1, j] = last_v

    @pl.when(j == 0)
    def _():
        step(True, False)

    @pl.when((j > 0) & (j < N_FF_TILES - 1))
    def _():
        step(False, False)

    @pl.when(j == N_FF_TILES - 1)
    def _():
        step(False, True)


def _ffn_sample_kernel(x_ref, g_ref, wg_ref, wv_ref, cwg_ref, cwv_ref, cbg_ref, cbv_ref, wd_ref, gf_ref,
                       pg_ref, pv_ref, y_ref, hg_ref, hv_ref, xn_ref, *, seq):
    j = pl.program_id(1)
    tm = x_ref.shape[0]
    _ffn_prologue(j, x_ref, g_ref, y_ref, xn_ref)
    hg = _dot(xn_ref[...], wg_ref[...])
    hv = _dot(xn_ref[...], wv_ref[...])
    hg_ref[...] = hg
    hv_ref[...] = hv
    pos = lax.broadcasted_iota(jnp.int32, (tm, FF_TILE), 0) % seq

    def delayed(h, hist_ref):
        def expand(t):
            n = hist_ref.shape[0]
            return jnp.broadcast_to(hist_ref[:, t:t + 1, :], (n, seq, FF_TILE)).reshape(tm, FF_TILE)

        def prev(k):
            rolled = pltpu.roll(h, k, axis=0)
            if k == 1:
                return jnp.where(pos == 0, expand(1), rolled)
            return jnp.where(pos == 0, expand(0), jnp.where(pos == 1, expand(1), rolled))
        return prev

    act = _conv_gate(hg, hv, delayed(hg, pg_ref), delayed(hv, pv_ref),
                     cwg_ref[...], cwv_ref[...], cbg_ref[...], cbv_ref[...])
    y_ref[...] += _dot(act, wd_ref[...])
    _ffn_epilogue(j, gf_ref, y_ref)


def _ffn_common_specs(tm):
    row = lambda i, j: (i, 0)
    fixed = lambda i, j: (0, 0)
    gate = lambda i, j: (0, j)
    val = lambda i, j: (0, N_FF_TILES + j)
    return [
        pl.BlockSpec((tm, D_MODEL), row),
        pl.BlockSpec((1, D_MODEL), fixed),
        pl.BlockSpec((D_MODEL, FF_TILE), gate),
        pl.BlockSpec((D_MODEL, FF_TILE), val),
        pl.BlockSpec((CONV_W, FF_TILE), gate),
        pl.BlockSpec((CONV_W, FF_TILE), val),
        pl.BlockSpec((1, FF_TILE), gate),
        pl.BlockSpec((1, FF_TILE), val),
        pl.BlockSpec((FF_TILE, D_MODEL), lambda i, j: (j, 0)),
        pl.BlockSpec((1, D_MODEL), fixed),
    ]


def _ffn_prompt(x, g, w_up, conv_w, conv_b, w_down, gf, *, tm, rows_per_seq):
    m = x.shape[0]
    blocks_per_seq = rows_per_seq // tm
    n_tiles = 2 * N_FF_TILES
    cw_tiles = conv_w.reshape(CONV_W, n_tiles, FF_TILE).transpose(1, 0, 2)
    cb_tiles = conv_b.reshape(n_tiles, 1, FF_TILE)
    row = lambda i, j: (i, 0)
    fixed2 = lambda i, j: (0, 0)
    fixed3 = lambda i, j: (0, 0, 0)
    last_shape = (1, 2, N_FF_TILES, CARRY_ROWS, FF_TILE)
    carry = pltpu.VMEM((N_FF_TILES, CARRY_ROWS, FF_TILE), F32)
    y, h_last = pl.pallas_call(
        functools.partial(_ffn_prompt_kernel, blocks_per_seq=blocks_per_seq),
        grid=(m // tm, N_FF_TILES),
        in_specs=[
            pl.BlockSpec((tm, D_MODEL), row),
            pl.BlockSpec((1, D_MODEL), fixed2),
            pl.BlockSpec((D_MODEL, FF_TILE), lambda i, j: (0, j)),
            pl.BlockSpec((D_MODEL, FF_TILE), lambda i, j: (0, N_FF_TILES + j)),
            pl.BlockSpec((n_tiles, CONV_W, FF_TILE), fixed3),
            pl.BlockSpec((n_tiles, 1, FF_TILE), fixed3),
            pl.BlockSpec((FF_TILE, D_MODEL), lambda i, j: (j, 0)),
            pl.BlockSpec((1, D_MODEL), fixed2),
        ],
        out_specs=[pl.BlockSpec((tm, D_MODEL), row), pl.BlockSpec(last_shape, lambda i, j: (i, 0, 0, 0, 0))],
        out_shape=[jax.ShapeDtypeStruct((m, D_MODEL), F32),
                   jax.ShapeDtypeStruct((m // tm,) + last_shape[1:], F32)],
        scratch_shapes=[pltpu.VMEM((tm, D_MODEL), BF16), carry, carry],
        compiler_params=pltpu.CompilerParams(
            dimension_semantics=("arbitrary", "arbitrary"), vmem_limit_bytes=VMEM_LIMIT_FFN),
        name="ffn_prompt",
    )(x, g, w_up, w_up, cw_tiles, cb_tiles, w_down, gf)
    tail = h_last[blocks_per_seq - 1::blocks_per_seq, :, :, CARRY_ROWS - (CONV_W - 1):, :]
    return y, tail.transpose(0, 3, 1, 2, 4).reshape(tail.shape[0], CONV_W - 1, 2 * D_FF)


def _ffn_sample(x, g, w_up, conv_w, conv_b, w_down, gf, hist, *, seq):
    m = x.shape[0]
    n_streams = m // seq
    hist_g = pl.BlockSpec((n_streams, CONV_W - 1, FF_TILE), lambda i, j: (0, 0, j))
    hist_v = pl.BlockSpec((n_streams, CONV_W - 1, FF_TILE), lambda i, j: (0, 0, N_FF_TILES + j))
    h_spec = pl.BlockSpec((m, FF_TILE), lambda i, j: (0, j))
    h_shape = jax.ShapeDtypeStruct((m, D_FF), F32)
    return pl.pallas_call(
        functools.partial(_ffn_sample_kernel, seq=seq),
        grid=(1, N_FF_TILES),
        in_specs=_ffn_common_specs(m) + [hist_g, hist_v],
        out_specs=[pl.BlockSpec((m, D_MODEL), lambda i, j: (0, 0)), h_spec, h_spec],
        out_shape=[jax.ShapeDtypeStruct((m, D_MODEL), F32), h_shape, h_shape],
        scratch_shapes=[pltpu.VMEM((m, D_MODEL), BF16)],
        compiler_params=pltpu.CompilerParams(
            dimension_semantics=("arbitrary", "arbitrary"), vmem_limit_bytes=VMEM_LIMIT),
        name="ffn_sample",
    )(x, g, w_up, w_up, conv_w, conv_w, conv_b, conv_b, w_down, gf, hist, hist)


def kernel(x_prompt, x_sample, cache_k, cache_v, cache_ffn_conv, norm_mix_g, w_in, rel_bias, sgu_norm_g, w_s, b_s,
           w_branch_a, w_branch_b, w_out, norm_ffn_g, w_up, conv_w, conv_b, w_down, norm_final_g):
    depth = w_in.shape[0]
    assert depth == 1, "single-layer trunk"
    batch, seq, _ = x_prompt.shape
    n_streams, n_new, _ = x_sample.shape
    n_cache = cache_k.shape[2]
    keep = min(KV_WINDOW, seq)

    row = lambda v: v.reshape(1, -1).astype(F32)
    w_in_bf = w_in[0].astype(BF16)
    g_mix, g_sgu, g_ffn, g_fin = row(norm_mix_g[0]), row(sgu_norm_g[0]), row(norm_ffn_g[0]), row(norm_final_g)
    cw, cb = conv_w[0].astype(F32), row(conv_b[0])

    xp = x_prompt.reshape(batch * seq, D_MODEL)
    hact_p, k_tail, v_tail, wa_bf, wb_bf, wo_bf, w_up_bf, w_down_bf = _in_proj(
        xp, g_mix, w_in_bf, g_sgu, tm=min(PROJ_ROWS, seq), rows_per_seq=seq, tail=keep, vb_tail=False,
        cast=(w_branch_a[0], w_branch_b[0], w_out[0], w_up[0], w_down[0]))
    a_p = _attn_prompt(hact_p, _prompt_bias(rel_bias[0]), batch=batch, seq=seq)
    wbd_p, sb_p = _sgu_block_weights(w_s[0], b_s[0], SGU_CHUNK)
    x1_p = _merge(xp, hact_p, a_p, wbd_p, sb_p, wa_bf, wb_bf, wo_bf, tm=MERGE_ROWS)
    ffn_rows = min(FFN_ROWS, seq)
    y_p, conv_p = _ffn_prompt(x1_p, g_ffn, w_up_bf, cw, cb, w_down_bf, g_fin, tm=ffn_rows, rows_per_seq=seq)

    ms = n_streams * n_new
    xs = x_sample.reshape(ms, D_MODEL)
    hact_s, k_new, v_new, vb_new = _in_proj(xs, g_mix, w_in_bf, g_sgu, tm=ms, rows_per_seq=ms, tail=ms,
                                            vb_tail=True)
    a_s = _attn_sample(hact_s, cache_k[0].reshape(n_streams, n_cache * N_HEADS, HEAD_DIM),
                       cache_v[0].reshape(n_streams, n_cache * N_HEADS, HEAD_DIM),
                       _sample_bias(rel_bias[0], n_cache, n_new), n_streams=n_streams, n_new=n_new)
    wbd_s, sb_s = _sgu_block_weights(w_s[0], b_s[0], n_new)
    x1_s = _merge(xs, hact_s, a_s, wbd_s, sb_s, wa_bf, wb_bf, wo_bf, tm=ms)
    y_s, h_g, h_v = _ffn_sample(x1_s, g_ffn, w_up_bf, cw, cb, w_down_bf, g_fin,
                                cache_ffn_conv[0], seq=n_new)

    hist = CONV_W - 1
    heads = lambda t, b, s: t.reshape(1, b, s, N_HEADS, HEAD_DIM)
    new_conv_prompt = conv_p[None]
    h_s = jnp.concatenate([h_g, h_v], axis=-1).reshape(n_streams, n_new, 2 * D_FF)
    new_conv_sample = h_s[:, n_new - hist:][None]
    return (
        y_p.reshape(batch, seq, D_MODEL),
        y_s.reshape(n_streams, n_new, D_MODEL),
        heads(k_tail, batch, keep),
        heads(v_tail, batch, keep),
        heads(k_new, n_streams, n_new),
        heads(v_new, n_streams, n_new),
        vb_new.reshape(1, n_streams, n_new, D_SGU),
        new_conv_prompt,
        new_conv_sample,
    )
```

```python
import functools

import numpy as np
import jax
import jax.numpy as jnp
from jax import lax
from jax.experimental import pallas as pl
from jax.experimental.pallas import tpu as pltpu

D_MODEL = 2048
CHUNK = 64
N_LEFT_CHUNKS = 8
KV_WINDOW = N_LEFT_CHUNKS * CHUNK
D_ATTN = D_MODEL // 2
N_HEADS = 8
HEAD_DIM = D_ATTN // N_HEADS
MAX_REL = 256
D_SGU = D_MODEL // 2
N_GROUPS = 8
GROUP_DIM = D_SGU // N_GROUPS
SGU_CHUNK = 128
D_FF = 5632
CONV_W = 3
EPS = 1e-6
PAST_LEN = 2048
D_IN = 3 * D_ATTN + 2 * D_SGU + 2 * D_MODEL
NEG_INF = -1e30

COL = 1024
COL_Q, COL_K, COL_V, COL_U, COL_VB, COL_GA, COL_GB = 0, 1, 2, 3, 4, 5, 7
N_COL_BLOCKS = D_IN // COL

PROJ_ROWS = 1024
MERGE_ROWS = 512
FFN_ROWS = 1024
FFN_SUB = 256
PROJ_COL_CHUNK = 512
PROJ_ROW_CHUNK = 256
ATTN_QB = 256
LOG2E = float(np.log2(np.e))
SGU_SUB = 256
FF_TILE = 512
N_FF_TILES = D_FF // FF_TILE
CARRY_ROWS = 8
BF16_SUBLANES = 16

VMEM_LIMIT = 56 * 1024 * 1024
VMEM_LIMIT_FFN = 60 * 1024 * 1024

BF16 = jnp.bfloat16
F32 = jnp.float32


def _rms(x, g):
    inv = lax.rsqrt(jnp.mean(x * x, axis=-1, keepdims=True) + EPS)
    return (x * inv) * g


def _gelu(x):
    return 0.5 * x * (1.0 + lax.erf(x * (2.0 ** -0.5)))


def _sigmoid(x):
    return 0.5 * jnp.tanh(0.5 * x) + 0.5


def _dot(a, b):
    return jnp.dot(a, b, preferred_element_type=F32)


def _dot_nt(a, b):
    return lax.dot_general(a, b, (((1,), (1,)), ((), ())), preferred_element_type=F32)


def _in_proj_kernel(x_ref, g_ref, w_ref, sg_ref, *refs, tail, vb_tail, n_cast):
    cast_in, refs = refs[:n_cast], refs[n_cast:]
    h_ref, kt_ref, vt_ref = refs[:3]
    vbt_ref = refs[3] if vb_tail else None
    cast_out, refs = refs[3 + vb_tail:3 + vb_tail + n_cast], refs[3 + vb_tail + n_cast:]
    wbf_ref = refs[0] if len(refs) == 2 else None
    xn_ref = refs[-1]
    j = pl.program_id(1)
    tm = x_ref.shape[0]
    head = tm - tail

    @pl.when(j == 0)
    def _():
        xn_ref[...] = _rms(x_ref[...], g_ref[...]).astype(BF16)

    def side_cast():
        for src, dst in zip(cast_in, cast_out):
            dst[...] = src[...].astype(BF16)

    def weights():
        if wbf_ref is None:
            return w_ref
        wbf_ref[...] = w_ref[...].astype(BF16)
        return wbf_ref

    def by_cols(act, tail_ref=None):
        side_cast()
        w_ref = weights()
        for c in range(COL // PROJ_COL_CHUNK):
            cols = slice(c * PROJ_COL_CHUNK, (c + 1) * PROJ_COL_CHUNK)
            acc = _dot(xn_ref[...], w_ref[:, cols])
            h_ref[:, cols] = act(acc).astype(BF16)
            if tail_ref is not None:
                tail_ref[:, cols] = acc[head:, :]

    @pl.when(j == COL_Q)
    def _():
        by_cols(lambda a: a)

    @pl.when(j == COL_K)
    def _():
        by_cols(lambda a: a, kt_ref)

    @pl.when(j == COL_V)
    def _():
        by_cols(lambda a: a, vt_ref)

    @pl.when(j == COL_U)
    def _():
        by_cols(_gelu)

    @pl.when(j == COL_VB)
    def _():
        side_cast()
        w_bf = weights()
        for r in range(tm // PROJ_ROW_CHUNK):
            lo = r * PROJ_ROW_CHUNK
            rows = slice(lo, lo + PROJ_ROW_CHUNK)
            vb = _rms(_gelu(_dot(xn_ref[rows, :], w_bf[...])), sg_ref[...])
            h_ref[rows, :] = vb.astype(BF16)
            if vbt_ref is not None and lo >= head:
                vbt_ref[lo - head:lo - head + PROJ_ROW_CHUNK, :] = vb

    @pl.when(j >= COL_GA)
    def _():
        by_cols(_sigmoid)


def _in_proj(x, g, w, sg, *, tm, rows_per_seq, tail, vb_tail, cast=()):
    m = x.shape[0]
    emit_w = w.dtype != BF16
    assert not emit_w or m == tm, "the weight copy is written once, so it needs a single row block"
    blocks_per_seq = rows_per_seq // tm
    n_seq = m // rows_per_seq
    assert tail <= tm and (tm - tail) % PROJ_ROW_CHUNK == 0
    n_tails = 3 if vb_tail else 2
    n_steps = (m // tm) * N_COL_BLOCKS
    tail_spec = pl.BlockSpec((tail, COL), lambda i, j: (i // blocks_per_seq, 0))
    tail_shape = jax.ShapeDtypeStruct((n_seq * tail, COL), F32)

    def slab_spec(w):
        rows = BF16_SUBLANES
        while w.shape[0] // rows > n_steps:
            rows *= 2
        assert w.shape[0] % rows == 0
        last = w.shape[0] // rows - 1
        return pl.BlockSpec((rows, w.shape[1]), lambda i, j: (jnp.minimum(i * N_COL_BLOCKS + j, last), 0))

    slab_specs = [slab_spec(c) for c in cast]
    w_spec = pl.BlockSpec((D_MODEL, COL), lambda i, j: (0, j))
    return pl.pallas_call(
        functools.partial(_in_proj_kernel, tail=tail, vb_tail=vb_tail, n_cast=len(cast)),
        grid=(m // tm, N_COL_BLOCKS),
        in_specs=[
            pl.BlockSpec((tm, D_MODEL), lambda i, j: (i, 0)),
            pl.BlockSpec((1, D_MODEL), lambda i, j: (0, 0)),
            w_spec,
            pl.BlockSpec((1, D_SGU), lambda i, j: (0, 0)),
        ] + slab_specs,
        out_specs=[pl.BlockSpec((tm, COL), lambda i, j: (i, j))] + [tail_spec] * n_tails + slab_specs
        + [w_spec] * emit_w,
        out_shape=[jax.ShapeDtypeStruct((m, D_IN), BF16)] + [tail_shape] * n_tails
        + [jax.ShapeDtypeStruct(c.shape, BF16) for c in cast] + [jax.ShapeDtypeStruct(w.shape, BF16)] * emit_w,
        scratch_shapes=[pltpu.VMEM((tm, D_MODEL), BF16)],
        compiler_params=pltpu.CompilerParams(
            dimension_semantics=("arbitrary", "arbitrary"), vmem_limit_bytes=VMEM_LIMIT),
        name="in_proj",
    )(x, g, w, sg, *cast)


def _attn_prompt_kernel(q_ref, k0_ref, k1_ref, k2_ref, v0_ref, v1_ref, v2_ref, diag_ref, o_ref, bias_ref):
    qb = pl.program_id(1)
    k_refs = (k0_ref, k1_ref, k2_ref)
    v_refs = (v0_ref, v1_ref, v2_ref)
    scale2 = HEAD_DIM ** -0.5 * LOG2E
    n_keys = 3 * ATTN_QB

    @pl.when((pl.program_id(0) == 0) & (qb == 0))
    def _():
        qi = lax.broadcasted_iota(jnp.int32, (ATTN_QB, n_keys), 0) // CHUNK
        ck = lax.broadcasted_iota(jnp.int32, (ATTN_QB, n_keys), 1) // CHUNK - (2 * ATTN_QB) // CHUNK
        allowed = (ck <= qi) & (qi - ck <= N_LEFT_CHUNKS)
        width = diag_ref.shape[1]
        for h in range(N_HEADS):
            row = jnp.broadcast_to(diag_ref[h:h + 1, :], (ATTN_QB, width))
            skew = pltpu.roll(row, width - (ATTN_QB - 1), 1, stride=1, stride_axis=0)
            bias_ref[h] = jnp.where(allowed, skew[:, :n_keys] * LOG2E, NEG_INF)

    ones = jnp.ones((3 * ATTN_QB, HEAD_DIM), BF16)

    def attend(mask_missing):
        for h in range(N_HEADS):
            cols = slice(h * HEAD_DIM, (h + 1) * HEAD_DIM)
            k = jnp.concatenate([r[:, cols] for r in k_refs], axis=0)
            v = jnp.concatenate([jnp.concatenate([r[:, cols] for r in v_refs], axis=0), ones], axis=1)
            s = _dot_nt(q_ref[:, cols], k) * scale2 + bias_ref[h]
            if mask_missing:
                s = jnp.concatenate(
                    [s[:, r * ATTN_QB:(r + 1) * ATTN_QB] + jnp.where(qb - 2 + r >= 0, 0.0, NEG_INF).astype(F32)
                     for r in range(3)], axis=1)
            e = jnp.exp2(s - s.max(-1, keepdims=True)).astype(BF16)
            o = _dot(e, v)
            o_ref[:, cols] = (o[:, :HEAD_DIM] / o[:, HEAD_DIM:]).astype(BF16)

    @pl.when(qb < KV_WINDOW // ATTN_QB)
    def _():
        attend(True)

    @pl.when(qb >= KV_WINDOW // ATTN_QB)
    def _():
        attend(False)


def _rel_bias_diagonals(rel_bias, n, m, d0):
    dist = np.arange(n + m - 1) - (m - 1) + d0
    diag = rel_bias[:, np.clip(dist, -MAX_REL, MAX_REL) + MAX_REL].astype(F32)
    rev = diag[:, ::-1]
    return jnp.concatenate([rev, rev[:, :1]], axis=1)


def _rel_bias_table(rel_bias, n, m, d0):
    length = n + m - 1
    padded = _rel_bias_diagonals(rel_bias, n, m, d0)
    skew = jnp.tile(padded, (1, n))[:, :n * length].reshape(-1, n, length)
    return skew[:, :, n - 1:n - 1 + m]


def _attn_prompt(hact, rel_bias, *, batch, seq):
    nqb = seq // ATTN_QB
    n_keys = 3 * ATTN_QB
    diag = _rel_bias_diagonals(rel_bias, ATTN_QB, n_keys, 2 * ATTN_QB)
    kv_spec = lambda col, r: pl.BlockSpec(
        (ATTN_QB, COL), lambda b, t: (b * nqb + jnp.maximum(t - 2 + r, 0), col))
    return pl.pallas_call(
        _attn_prompt_kernel,
        grid=(batch, nqb),
        in_specs=[pl.BlockSpec((ATTN_QB, COL), lambda b, t: (b * nqb + t, COL_Q))]
        + [kv_spec(COL_K, r) for r in range(3)] + [kv_spec(COL_V, r) for r in range(3)]
        + [pl.BlockSpec(diag.shape, lambda b, t: (0, 0))],
        out_specs=pl.BlockSpec((ATTN_QB, D_ATTN), lambda b, t: (b * nqb + t, 0)),
        out_shape=jax.ShapeDtypeStruct((batch * seq, D_ATTN), BF16),
        scratch_shapes=[pltpu.VMEM((N_HEADS, ATTN_QB, n_keys), F32)],
        compiler_params=pltpu.CompilerParams(
            dimension_semantics=("arbitrary", "arbitrary"), vmem_limit_bytes=VMEM_LIMIT),
        name="attn_prompt",
    )(hact, hact, hact, hact, hact, hact, hact, diag)


def _attn_sample_kernel(q_ref, kn_ref, vn_ref, kc_ref, vc_ref, bias_ref, o_ref):
    n_cache = kc_ref.shape[1] // N_HEADS
    scale = HEAD_DIM ** -0.5
    for h in range(N_HEADS):
        cols = slice(h * HEAD_DIM, (h + 1) * HEAD_DIM)
        head_rows = pl.ds(h, n_cache, stride=N_HEADS)
        q = q_ref[:, cols]
        s_c = _dot_nt(q, kc_ref[0, head_rows, :].astype(BF16)) * scale + bias_ref[h, :, :n_cache]
        s_n = _dot_nt(q, kn_ref[:, cols]) * scale + bias_ref[h, :, n_cache:]
        mx = jnp.maximum(s_c.max(-1, keepdims=True), s_n.max(-1, keepdims=True))
        e_c, e_n = jnp.exp(s_c - mx), jnp.exp(s_n - mx)
        den = e_c.sum(-1, keepdims=True) + e_n.sum(-1, keepdims=True)
        o = (_dot(e_c.astype(BF16), vc_ref[0, head_rows, :].astype(BF16))
             + _dot(e_n.astype(BF16), vn_ref[:, cols]))
        o_ref[:, cols] = (o / den).astype(BF16)


def _sample_bias(rel_bias, n_cache, n_new):
    q_pos = PAST_LEN + np.arange(n_new)
    k_pos = np.concatenate([PAST_LEN - n_cache + np.arange(n_cache), PAST_LEN + np.arange(n_new)])
    cq, ck = q_pos[:, None] // CHUNK, k_pos[None, :] // CHUNK
    allowed = (ck <= cq) & (cq - ck <= N_LEFT_CHUNKS)
    bias = jnp.concatenate([_rel_bias_table(rel_bias, n_new, n_cache, n_cache),
                            _rel_bias_table(rel_bias, n_new, n_new, 0)], axis=2)
    return jnp.where(allowed[None], bias, NEG_INF)


def _attn_sample(hact, cache_k, cache_v, bias, *, n_streams, n_new):
    n_cache = cache_k.shape[1] // N_HEADS
    new_spec = lambda col: pl.BlockSpec((n_new, COL), lambda b: (b, col))
    cache_spec = pl.BlockSpec((1, n_cache * N_HEADS, HEAD_DIM), lambda b: (b, 0, 0))
    return pl.pallas_call(
        _attn_sample_kernel,
        grid=(n_streams,),
        in_specs=[new_spec(COL_Q), new_spec(COL_K), new_spec(COL_V), cache_spec, cache_spec,
                  pl.BlockSpec((N_HEADS, n_new, n_cache + n_new), lambda b: (0, 0, 0))],
        out_specs=pl.BlockSpec((n_new, D_ATTN), lambda b: (b, 0)),
        out_shape=jax.ShapeDtypeStruct((n_streams * n_new, D_ATTN), BF16),
        compiler_params=pltpu.CompilerParams(
            dimension_semantics=("arbitrary",), vmem_limit_bytes=VMEM_LIMIT),
        name="attn_sample",
    )(hact, hact, hact, cache_k, cache_v, bias)


def _merge_kernel(x_ref, u_ref, vb_ref, ga0_ref, ga1_ref, gb0_ref, gb1_ref, a_ref,
                  wbd_ref, sb_ref, wa_ref, wb_ref, wo_ref, o_ref, s_ref):
    tm = x_ref.shape[0]
    for c in range(tm // SGU_SUB):
        rows = slice(c * SGU_SUB, (c + 1) * SGU_SUB)
        for g in range(N_GROUPS):
            cols = slice(g * GROUP_DIM, (g + 1) * GROUP_DIM)
            mixed = _dot(wbd_ref[g], vb_ref[rows, cols]) + sb_ref[:, cols]
            s_ref[rows, cols] = (u_ref[rows, cols].astype(F32) * mixed).astype(BF16)
    pa = _dot(a_ref[...], wa_ref[...])
    pb = _dot(s_ref[...], wb_ref[...])
    half = D_MODEL // 2
    m0 = ga0_ref[...].astype(F32) * pa[:, :half] + gb0_ref[...].astype(F32) * pb[:, :half]
    m1 = ga1_ref[...].astype(F32) * pa[:, half:] + gb1_ref[...].astype(F32) * pb[:, half:]
    m = jnp.concatenate([m0, m1], axis=-1).astype(BF16)
    o_ref[...] = x_ref[...] + _dot(m, wo_ref[...])


def _sgu_block_weights(w_s, b_s, chunk):
    reps = SGU_SUB // chunk
    w = (w_s * jnp.tril(jnp.ones((SGU_CHUNK, SGU_CHUNK), w_s.dtype)))[:, :chunk, :chunk]
    eye = jnp.eye(reps, dtype=w.dtype)
    wbd = (eye[None, :, None, :, None] * w[:, None, :, None, :]).reshape(N_GROUPS, SGU_SUB, SGU_SUB)
    bias = jnp.repeat(jnp.tile(b_s[:, :chunk].T, (reps, 1)), GROUP_DIM, axis=1)
    return wbd.astype(BF16), bias.astype(F32)


def _merge(x, hact, a, wbd, sbias, wa, wb, wo, *, tm):
    m = x.shape[0]
    hcol = lambda col: pl.BlockSpec((tm, COL), lambda i: (i, col))
    const = lambda shape: pl.BlockSpec(shape, lambda i: (0,) * len(shape), pipeline_mode=pl.Buffered(1))
    return pl.pallas_call(
        _merge_kernel,
        grid=(m // tm,),
        in_specs=[pl.BlockSpec((tm, D_MODEL), lambda i: (i, 0)),
                  hcol(COL_U), hcol(COL_VB), hcol(COL_GA), hcol(COL_GA + 1), hcol(COL_GB), hcol(COL_GB + 1),
                  pl.BlockSpec((tm, D_ATTN), lambda i: (i, 0)),
                  const((N_GROUPS, SGU_SUB, SGU_SUB)), const((SGU_SUB, D_SGU)),
                  const((D_ATTN, D_MODEL)), const((D_SGU, D_MODEL)), const((D_MODEL, D_MODEL))],
        out_specs=pl.BlockSpec((tm, D_MODEL), lambda i: (i, 0)),
        out_shape=jax.ShapeDtypeStruct((m, D_MODEL), F32),
        scratch_shapes=[pltpu.VMEM((tm, D_SGU), BF16)],
        compiler_params=pltpu.CompilerParams(
            dimension_semantics=("arbitrary",), vmem_limit_bytes=VMEM_LIMIT),
        name="merge",
    )(x, hact, hact, hact, hact, hact, hact, a, wbd, sbias, wa, wb, wo)


def _conv_gate(hg, hv, prev_g, prev_v, cwg, cwv, cbg, cbv):
    def conv(h, prev, cw, cb):
        return cb + cw[0:1, :] * prev(2) + cw[1:2, :] * prev(1) + cw[2:3, :] * h
    return (_gelu(conv(hg, prev_g, cwg, cbg)) * conv(hv, prev_v, cwv, cbv)).astype(BF16)


def _ffn_prologue(j, x_ref, g_ref, y_ref, xn_ref):
    @pl.when(j == 0)
    def _():
        x = x_ref[...]
        xn_ref[...] = _rms(x, g_ref[...]).astype(BF16)
        y_ref[...] = x


def _ffn_epilogue(j, gf_ref, y_ref):
    @pl.when(j == N_FF_TILES - 1)
    def _():
        y_ref[...] = _rms(y_ref[...], gf_ref[...])


def _ffn_prompt_kernel(x_ref, g_ref, wg_ref, wv_ref, cw_ref, cb_ref, wd_ref, gf_ref,
                       y_ref, hl_ref, xn_ref, cg_ref, cv_ref, *, blocks_per_seq):
    i = pl.program_id(0)
    j = pl.program_id(1)
    tm = x_ref.shape[0]

    @pl.when(i % blocks_per_seq == 0)
    def _():
        cg_ref[j] = jnp.zeros(cg_ref.shape[1:], F32)
        cv_ref[j] = jnp.zeros(cv_ref.shape[1:], F32)

    def delayed(h, carry):
        def prev(k):
            head = jnp.concatenate([carry, h[:CARRY_ROWS]], axis=0)[CARRY_ROWS - k:2 * CARRY_ROWS - k]
            return jnp.concatenate([head, pltpu.roll(h, k, axis=0)[CARRY_ROWS:]], axis=0)
        return prev

    def step(first, last):
        if first:
            xn_ref[...] = _rms(x_ref[...], g_ref[...]).astype(BF16)
        up = lambda w_ref: jnp.concatenate(
            [_dot(xn_ref[r:r + FFN_SUB, :], w_ref[...]) for r in range(0, tm, FFN_SUB)], axis=0)
        hg = up(wg_ref)
        hv = up(wv_ref)
        act = _conv_gate(hg, hv, delayed(hg, cg_ref[j]), delayed(hv, cv_ref[j]),
                         cw_ref[j], cw_ref[N_FF_TILES + j], cb_ref[j], cb_ref[N_FF_TILES + j])
        y = (x_ref if first else y_ref)[...] + _dot(act, wd_ref[...])
        y_ref[...] = _rms(y, gf_ref[...]) if last else y
        last_g, last_v = hg[tm - CARRY_ROWS:], hv[tm - CARRY_ROWS:]
        cg_ref[j] = last_g
        cv_ref[j] = last_v
        hl_ref[0, 0, j] = last_g
        hl_ref[0, 1, j] = last_v

    @pl.when(j == 0)
    def _():
        step(True, False)

    @pl.when((j > 0) & (j < N_FF_TILES - 1))
    def _():
        step(False, False)

    @pl.when(j == N_FF_TILES - 1)
    def _():
        step(False, True)


def _ffn_sample_kernel(x_ref, g_ref, wg_ref, wv_ref, cwg_ref, cwv_ref, cbg_ref, cbv_ref, wd_ref, gf_ref,
                       pg_ref, pv_ref, y_ref, hg_ref, hv_ref, xn_ref, *, seq):
    j = pl.program_id(1)
    tm = x_ref.shape[0]
    _ffn_prologue(j, x_ref, g_ref, y_ref, xn_ref)
    hg = _dot(xn_ref[...], wg_ref[...])
    hv = _dot(xn_ref[...], wv_ref[...])
    hg_ref[...] = hg
    hv_ref[...] = hv
    pos = lax.broadcasted_iota(jnp.int32, (tm, FF_TILE), 0) % seq

    def delayed(h, hist_ref):
        def expand(t):
            n = hist_ref.shape[0]
            return jnp.broadcast_to(hist_ref[:, t:t + 1, :], (n, seq, FF_TILE)).reshape(tm, FF_TILE)

        def prev(k):
            rolled = pltpu.roll(h, k, axis=0)
            if k == 1:
                return jnp.where(pos == 0, expand(1), rolled)
            return jnp.where(pos == 0, expand(0), jnp.where(pos == 1, expand(1), rolled))
        return prev

    act = _conv_gate(hg, hv, delayed(hg, pg_ref), delayed(hv, pv_ref),
                     cwg_ref[...], cwv_ref[...], cbg_ref[...], cbv_ref[...])
    y_ref[...] += _dot(act, wd_ref[...])
    _ffn_epilogue(j, gf_ref, y_ref)


def _ffn_common_specs(tm):
    row = lambda i, j: (i, 0)
    fixed = lambda i, j: (0, 0)
    gate = lambda i, j: (0, j)
    val = lambda i, j: (0, N_FF_TILES + j)
    return [
        pl.BlockSpec((tm, D_MODEL), row),
        pl.BlockSpec((1, D_MODEL), fixed),
        pl.BlockSpec((D_MODEL, FF_TILE), gate),
        pl.BlockSpec((D_MODEL, FF_TILE), val),
        pl.BlockSpec((CONV_W, FF_TILE), gate),
        pl.BlockSpec((CONV_W, FF_TILE), val),
        pl.BlockSpec((1, FF_TILE), gate),
        pl.BlockSpec((1, FF_TILE), val),
        pl.BlockSpec((FF_TILE, D_MODEL), lambda i, j: (j, 0)),
        pl.BlockSpec((1, D_MODEL), fixed),
    ]


def _ffn_prompt(x, g, w_up, conv_w, conv_b, w_down, gf, *, tm, rows_per_seq):
    m = x.shape[0]
    blocks_per_seq = rows_per_seq // tm
    n_tiles = 2 * N_FF_TILES
    cw_tiles = conv_w.reshape(CONV_W, n_tiles, FF_TILE).transpose(1, 0, 2)
    cb_tiles = conv_b.reshape(n_tiles, 1, FF_TILE)
    row = lambda i, j: (i, 0)
    fixed2 = lambda i, j: (0, 0)
    fixed3 = lambda i, j: (0, 0, 0)
    last_shape = (1, 2, N_FF_TILES, CARRY_ROWS, FF_TILE)
    carry = pltpu.VMEM((N_FF_TILES, CARRY_ROWS, FF_TILE), F32)
    y, h_last = pl.pallas_call(
        functools.partial(_ffn_prompt_kernel, blocks_per_seq=blocks_per_seq),
        grid=(m // tm, N_FF_TILES),
        in_specs=[
            pl.BlockSpec((tm, D_MODEL), row),
            pl.BlockSpec((1, D_MODEL), fixed2),
            pl.BlockSpec((D_MODEL, FF_TILE), lambda i, j: (0, j)),
            pl.BlockSpec((D_MODEL, FF_TILE), lambda i, j: (0, N_FF_TILES + j)),
            pl.BlockSpec((n_tiles, CONV_W, FF_TILE), fixed3),
            pl.BlockSpec((n_tiles, 1, FF_TILE), fixed3),
            pl.BlockSpec((FF_TILE, D_MODEL), lambda i, j: (j, 0)),
            pl.BlockSpec((1, D_MODEL), fixed2),
        ],
        out_specs=[pl.BlockSpec((tm, D_MODEL), row), pl.BlockSpec(last_shape, lambda i, j: (i, 0, 0, 0, 0))],
        out_shape=[jax.ShapeDtypeStruct((m, D_MODEL), F32),
                   jax.ShapeDtypeStruct((m // tm,) + last_shape[1:], F32)],
        scratch_shapes=[pltpu.VMEM((tm, D_MODEL), BF16), carry, carry],
        compiler_params=pltpu.CompilerParams(
            dimension_semantics=("arbitrary", "arbitrary"), vmem_limit_bytes=VMEM_LIMIT_FFN),
        name="ffn_prompt",
    )(x, g, w_up, w_up, cw_tiles, cb_tiles, w_down, gf)
    tail = h_last[blocks_per_seq - 1::blocks_per_seq, :, :, CARRY_ROWS - (CONV_W - 1):, :]
    return y, tail.transpose(0, 3, 1, 2, 4).reshape(tail.shape[0], CONV_W - 1, 2 * D_FF)


def _ffn_sample(x, g, w_up, conv_w, conv_b, w_down, gf, hist, *, seq):
    m = x.shape[0]
    n_streams = m // seq
    hist_g = pl.BlockSpec((n_streams, CONV_W - 1, FF_TILE), lambda i, j: (0, 0, j))
    hist_v = pl.BlockSpec((n_streams, CONV_W - 1, FF_TILE), lambda i, j: (0, 0, N_FF_TILES + j))
    h_spec = pl.BlockSpec((m, FF_TILE), lambda i, j: (0, j))
    h_shape = jax.ShapeDtypeStruct((m, D_FF), F32)
    return pl.pallas_call(
        functools.partial(_ffn_sample_kernel, seq=seq),
        grid=(1, N_FF_TILES),
        in_specs=_ffn_common_specs(m) + [hist_g, hist_v],
        out_specs=[pl.BlockSpec((m, D_MODEL), lambda i, j: (0, 0)), h_spec, h_spec],
        out_shape=[jax.ShapeDtypeStruct((m, D_MODEL), F32), h_shape, h_shape],
        scratch_shapes=[pltpu.VMEM((m, D_MODEL), BF16)],
        compiler_params=pltpu.CompilerParams(
            dimension_semantics=("arbitrary", "arbitrary"), vmem_limit_bytes=VMEM_LIMIT),
        name="ffn_sample",
    )(x, g, w_up, w_up, conv_w, conv_w, conv_b, conv_b, w_down, gf, hist, hist)


def kernel(x_prompt, x_sample, cache_k, cache_v, cache_ffn_conv, norm_mix_g, w_in, rel_bias, sgu_norm_g, w_s, b_s,
           w_branch_a, w_branch_b, w_out, norm_ffn_g, w_up, conv_w, conv_b, w_down, norm_final_g):
    depth = w_in.shape[0]
    assert depth == 1, "single-layer trunk"
    batch, seq, _ = x_prompt.shape
    n_streams, n_new, _ = x_sample.shape
    n_cache = cache_k.shape[2]
    keep = min(KV_WINDOW, seq)

    row = lambda v: v.reshape(1, -1).astype(F32)
    g_mix, g_sgu, g_ffn, g_fin = row(norm_mix_g[0]), row(sgu_norm_g[0]), row(norm_ffn_g[0]), row(norm_final_g)
    cw, cb = conv_w[0].astype(F32), row(conv_b[0])

    ms = n_streams * n_new
    xs = x_sample.reshape(ms, D_MODEL)
    hact_s, k_new, v_new, vb_new, w_in_bf = _in_proj(xs, g_mix, w_in[0], g_sgu, tm=ms, rows_per_seq=ms, tail=ms,
                                                     vb_tail=True)

    xp = x_prompt.reshape(batch * seq, D_MODEL)
    hact_p, k_tail, v_tail, wa_bf, wb_bf, wo_bf, w_up_bf, w_down_bf = _in_proj(
        xp, g_mix, w_in_bf, g_sgu, tm=min(PROJ_ROWS, seq), rows_per_seq=seq, tail=keep, vb_tail=False,
        cast=(w_branch_a[0], w_branch_b[0], w_out[0], w_up[0], w_down[0]))
    a_p = _attn_prompt(hact_p, rel_bias[0], batch=batch, seq=seq)
    wbd_p, sb_p = _sgu_block_weights(w_s[0], b_s[0], SGU_CHUNK)
    x1_p = _merge(xp, hact_p, a_p, wbd_p, sb_p, wa_bf, wb_bf, wo_bf, tm=MERGE_ROWS)
    ffn_rows = min(FFN_ROWS, seq)
    y_p, conv_p = _ffn_prompt(x1_p, g_ffn, w_up_bf, cw, cb, w_down_bf, g_fin, tm=ffn_rows, rows_per_seq=seq)

    a_s = _attn_sample(hact_s, cache_k[0].reshape(n_streams, n_cache * N_HEADS, HEAD_DIM),
                       cache_v[0].reshape(n_streams, n_cache * N_HEADS, HEAD_DIM),
                       _sample_bias(rel_bias[0], n_cache, n_new), n_streams=n_streams, n_new=n_new)
    wbd_s, sb_s = _sgu_block_weights(w_s[0], b_s[0], n_new)
    x1_s = _merge(xs, hact_s, a_s, wbd_s, sb_s, wa_bf, wb_bf, wo_bf, tm=ms)
    y_s, h_g, h_v = _ffn_sample(x1_s, g_ffn, w_up_bf, cw, cb, w_down_bf, g_fin,
                                cache_ffn_conv[0], seq=n_new)

    hist = CONV_W - 1
    heads = lambda t, b, s: t.reshape(1, b, s, N_HEADS, HEAD_DIM)
    new_conv_prompt = conv_p[None]
    h_s = jnp.concatenate([h_g, h_v], axis=-1).reshape(n_streams, n_new, 2 * D_FF)
    new_conv_sample = h_s[:, n_new - hist:][None]
    return (
        y_p.reshape(batch, seq, D_MODEL),
        y_s.reshape(n_streams, n_new, D_MODEL),
        heads(k_tail, batch, keep),
        heads(v_tail, batch, keep),
        heads(k_new, n_streams, n_new),
        heads(v_new, n_streams, n_new),
        vb_new.reshape(1, n_streams, n_new, D_SGU),
        new_conv_prompt,
        new_conv_sample,
    )
```

```python
import functools

import numpy as np
import jax
import jax.numpy as jnp
from jax import lax
from jax.experimental import pallas as pl
from jax.experimental.pallas import tpu as pltpu

D_MODEL = 2048
CHUNK = 64
N_LEFT_CHUNKS = 8
KV_WINDOW = N_LEFT_CHUNKS * CHUNK
D_ATTN = D_MODEL // 2
N_HEADS = 8
HEAD_DIM = D_ATTN // N_HEADS
MAX_REL = 256
D_SGU = D_MODEL // 2
N_GROUPS = 8
GROUP_DIM = D_SGU // N_GROUPS
SGU_CHUNK = 128
D_FF = 5632
CONV_W = 3
EPS = 1e-6
PAST_LEN = 2048
D_IN = 3 * D_ATTN + 2 * D_SGU + 2 * D_MODEL
NEG_INF = -1e30

COL = 1024
COL_Q, COL_K, COL_V, COL_U, COL_VB, COL_GA, COL_GB = 0, 1, 2, 3, 4, 5, 7
N_COL_BLOCKS = D_IN // COL

PROJ_ROWS = 1024
MERGE_ROWS = 512
FFN_ROWS = 1024
FFN_SUB = 256
PROJ_COL_CHUNK = 512
PROJ_ROW_CHUNK = 256
ATTN_QB = 256
LOG2E = float(np.log2(np.e))
SGU_SUB = 256
FF_TILE = 512
N_FF_TILES = D_FF // FF_TILE
CARRY_ROWS = 8
BF16_SUBLANES = 16

VMEM_LIMIT = 56 * 1024 * 1024
VMEM_LIMIT_FFN = 60 * 1024 * 1024

BF16 = jnp.bfloat16
F32 = jnp.float32


def _rms(x, g):
    inv = lax.rsqrt(jnp.mean(x * x, axis=-1, keepdims=True) + EPS)
    return (x * inv) * g


def _gelu(x):
    return 0.5 * x * (1.0 + lax.erf(x * (2.0 ** -0.5)))


def _sigmoid(x):
    return 0.5 * jnp.tanh(0.5 * x) + 0.5


def _dot(a, b):
    return jnp.dot(a, b, preferred_element_type=F32)


def _dot_nt(a, b):
    return lax.dot_general(a, b, (((1,), (1,)), ((), ())), preferred_element_type=F32)


def _in_proj_kernel(x_ref, g_ref, w_ref, sg_ref, *refs, tail, vb_tail, n_cast):
    cast_in, refs = refs[:n_cast], refs[n_cast:]
    h_ref, kt_ref, vt_ref = refs[:3]
    vbt_ref = refs[3] if vb_tail else None
    cast_out, refs = refs[3 + vb_tail:3 + vb_tail + n_cast], refs[3 + vb_tail + n_cast:]
    wbf_ref = refs[0] if len(refs) == 2 else None
    xn_ref = refs[-1]
    j = pl.program_id(1)
    tm = x_ref.shape[0]
    head = tm - tail

    @pl.when(j == 0)
    def _():
        xn_ref[...] = _rms(x_ref[...], g_ref[...]).astype(BF16)

    def side_cast():
        for src, dst in zip(cast_in, cast_out):
            dst[...] = src[...].astype(BF16)

    def weights():
        if wbf_ref is None:
            return w_ref
        wbf_ref[...] = w_ref[...].astype(BF16)
        return wbf_ref

    def by_cols(act, tail_ref=None):
        side_cast()
        w_ref = weights()
        for c in range(COL // PROJ_COL_CHUNK):
            cols = slice(c * PROJ_COL_CHUNK, (c + 1) * PROJ_COL_CHUNK)
            acc = _dot(xn_ref[...], w_ref[:, cols])
            h_ref[:, cols] = act(acc).astype(BF16)
            if tail_ref is not None:
                tail_ref[:, cols] = acc[head:, :]

    @pl.when(j == COL_Q)
    def _():
        by_cols(lambda a: a)

    @pl.when(j == COL_K)
    def _():
        by_cols(lambda a: a, kt_ref)

    @pl.when(j == COL_V)
    def _():
        by_cols(lambda a: a, vt_ref)

    @pl.when(j == COL_U)
    def _():
        by_cols(_gelu)

    @pl.when(j == COL_VB)
    def _():
        side_cast()
        w_bf = weights()
        for r in range(tm // PROJ_ROW_CHUNK):
            lo = r * PROJ_ROW_CHUNK
            rows = slice(lo, lo + PROJ_ROW_CHUNK)
            vb = _rms(_gelu(_dot(xn_ref[rows, :], w_bf[...])), sg_ref[...])
            h_ref[rows, :] = vb.astype(BF16)
            if vbt_ref is not None and lo >= head:
                vbt_ref[lo - head:lo - head + PROJ_ROW_CHUNK, :] = vb

    @pl.when(j >= COL_GA)
    def _():
        by_cols(_sigmoid)


def _in_proj(x, g, w, sg, *, tm, rows_per_seq, tail, vb_tail, cast=()):
    m = x.shape[0]
    emit_w = w.dtype != BF16
    assert not emit_w or m == tm, "the weight copy is written once, so it needs a single row block"
    blocks_per_seq = rows_per_seq // tm
    n_seq = m // rows_per_seq
    assert tail <= tm and (tm - tail) % PROJ_ROW_CHUNK == 0
    n_tails = 3 if vb_tail else 2
    n_steps = (m // tm) * N_COL_BLOCKS
    tail_spec = pl.BlockSpec((tail, COL), lambda i, j: (i // blocks_per_seq, 0))
    tail_shape = jax.ShapeDtypeStruct((n_seq * tail, COL), F32)

    def slab_spec(w):
        rows = BF16_SUBLANES
        while w.shape[0] // rows > n_steps:
            rows *= 2
        assert w.shape[0] % rows == 0
        last = w.shape[0] // rows - 1
        return pl.BlockSpec((rows, w.shape[1]), lambda i, j: (jnp.minimum(i * N_COL_BLOCKS + j, last), 0))

    slab_specs = [slab_spec(c) for c in cast]
    w_spec = pl.BlockSpec((D_MODEL, COL), lambda i, j: (0, j))
    return pl.pallas_call(
        functools.partial(_in_proj_kernel, tail=tail, vb_tail=vb_tail, n_cast=len(cast)),
        grid=(m // tm, N_COL_BLOCKS),
        in_specs=[
            pl.BlockSpec((tm, D_MODEL), lambda i, j: (i, 0)),
            pl.BlockSpec((1, D_MODEL), lambda i, j: (0, 0)),
            w_spec,
            pl.BlockSpec((1, D_SGU), lambda i, j: (0, 0)),
        ] + slab_specs,
        out_specs=[pl.BlockSpec((tm, COL), lambda i, j: (i, j))] + [tail_spec] * n_tails + slab_specs
        + [w_spec] * emit_w,
        out_shape=[jax.ShapeDtypeStruct((m, D_IN), BF16)] + [tail_shape] * n_tails
        + [jax.ShapeDtypeStruct(c.shape, BF16) for c in cast] + [jax.ShapeDtypeStruct(w.shape, BF16)] * emit_w,
        scratch_shapes=[pltpu.VMEM((tm, D_MODEL), BF16)],
        compiler_params=pltpu.CompilerParams(
            dimension_semantics=("arbitrary", "arbitrary"), vmem_limit_bytes=VMEM_LIMIT),
        name="in_proj",
    )(x, g, w, sg, *cast)


def _attn_prompt_kernel(q_ref, k0_ref, k1_ref, k2_ref, v0_ref, v1_ref, v2_ref, diag_ref, o_ref, bias_ref):
    qb = pl.program_id(1)
    k_refs = (k0_ref, k1_ref, k2_ref)
    v_refs = (v0_ref, v1_ref, v2_ref)
    scale2 = HEAD_DIM ** -0.5 * LOG2E
    n_keys = 3 * ATTN_QB

    @pl.when((pl.program_id(0) == 0) & (qb == 0))
    def _():
        qi = lax.broadcasted_iota(jnp.int32, (ATTN_QB, n_keys), 0) // CHUNK
        ck = lax.broadcasted_iota(jnp.int32, (ATTN_QB, n_keys), 1) // CHUNK - (2 * ATTN_QB) // CHUNK
        allowed = (ck <= qi) & (qi - ck <= N_LEFT_CHUNKS)
        width = diag_ref.shape[1]
        for h in range(N_HEADS):
            row = jnp.broadcast_to(diag_ref[h:h + 1, :], (ATTN_QB, width))
            skew = pltpu.roll(row, width - (ATTN_QB - 1), 1, stride=1, stride_axis=0)
            bias_ref[h] = jnp.where(allowed, skew[:, :n_keys] * LOG2E, NEG_INF)

    ones = jnp.ones((3 * ATTN_QB, HEAD_DIM), BF16)

    def attend(mask_missing):
        for h in range(N_HEADS):
            cols = slice(h * HEAD_DIM, (h + 1) * HEAD_DIM)
            k = jnp.concatenate([r[:, cols] for r in k_refs], axis=0)
            v = jnp.concatenate([jnp.concatenate([r[:, cols] for r in v_refs], axis=0), ones], axis=1)
            s = _dot_nt(q_ref[:, cols], k) * scale2 + bias_ref[h]
            if mask_missing:
                s = jnp.concatenate(
                    [s[:, r * ATTN_QB:(r + 1) * ATTN_QB] + jnp.where(qb - 2 + r >= 0, 0.0, NEG_INF).astype(F32)
                     for r in range(3)], axis=1)
            e = jnp.exp2(s - s.max(-1, keepdims=True)).astype(BF16)
            o = _dot(e, v)
            o_ref[:, cols] = (o[:, :HEAD_DIM] / o[:, HEAD_DIM:]).astype(BF16)

    @pl.when(qb < KV_WINDOW // ATTN_QB)
    def _():
        attend(True)

    @pl.when(qb >= KV_WINDOW // ATTN_QB)
    def _():
        attend(False)


def _rel_bias_diagonals(rel_bias, n, m, d0):
    dist = np.arange(n + m - 1) - (m - 1) + d0
    diag = rel_bias[:, np.clip(dist, -MAX_REL, MAX_REL) + MAX_REL].astype(F32)
    rev = diag[:, ::-1]
    return jnp.concatenate([rev, rev[:, :1]], axis=1)


def _rel_bias_table(rel_bias, n, m, d0):
    length = n + m - 1
    padded = _rel_bias_diagonals(rel_bias, n, m, d0)
    skew = jnp.tile(padded, (1, n))[:, :n * length].reshape(-1, n, length)
    return skew[:, :, n - 1:n - 1 + m]


def _attn_prompt(hact, rel_bias, *, batch, seq):
    nqb = seq // ATTN_QB
    n_keys = 3 * ATTN_QB
    diag = _rel_bias_diagonals(rel_bias, ATTN_QB, n_keys, 2 * ATTN_QB)
    kv_spec = lambda col, r: pl.BlockSpec(
        (ATTN_QB, COL), lambda b, t: (b * nqb + jnp.maximum(t - 2 + r, 0), col))
    return pl.pallas_call(
        _attn_prompt_kernel,
        grid=(batch, nqb),
        in_specs=[pl.BlockSpec((ATTN_QB, COL), lambda b, t: (b * nqb + t, COL_Q))]
        + [kv_spec(COL_K, r) for r in range(3)] + [kv_spec(COL_V, r) for r in range(3)]
        + [pl.BlockSpec(diag.shape, lambda b, t: (0, 0))],
        out_specs=pl.BlockSpec((ATTN_QB, D_ATTN), lambda b, t: (b * nqb + t, 0)),
        out_shape=jax.ShapeDtypeStruct((batch * seq, D_ATTN), BF16),
        scratch_shapes=[pltpu.VMEM((N_HEADS, ATTN_QB, n_keys), F32)],
        compiler_params=pltpu.CompilerParams(
            dimension_semantics=("arbitrary", "arbitrary"), vmem_limit_bytes=VMEM_LIMIT),
        name="attn_prompt",
    )(hact, hact, hact, hact, hact, hact, hact, diag)


def _attn_sample_kernel(q_ref, kn_ref, vn_ref, kc_ref, vc_ref, bias_ref, o_ref):
    n_cache = kc_ref.shape[1] // N_HEADS
    scale = HEAD_DIM ** -0.5
    for h in range(N_HEADS):
        cols = slice(h * HEAD_DIM, (h + 1) * HEAD_DIM)
        head_rows = pl.ds(h, n_cache, stride=N_HEADS)
        q = q_ref[:, cols]
        s_c = _dot_nt(q, kc_ref[0, head_rows, :].astype(BF16)) * scale + bias_ref[h, :, :n_cache]
        s_n = _dot_nt(q, kn_ref[:, cols]) * scale + bias_ref[h, :, n_cache:]
        mx = jnp.maximum(s_c.max(-1, keepdims=True), s_n.max(-1, keepdims=True))
        e_c, e_n = jnp.exp(s_c - mx), jnp.exp(s_n - mx)
        den = e_c.sum(-1, keepdims=True) + e_n.sum(-1, keepdims=True)
        o = (_dot(e_c.astype(BF16), vc_ref[0, head_rows, :].astype(BF16))
             + _dot(e_n.astype(BF16), vn_ref[:, cols]))
        o_ref[:, cols] = (o / den).astype(BF16)


def _sample_bias(rel_bias, n_cache, n_new):
    q_pos = PAST_LEN + np.arange(n_new)
    k_pos = np.concatenate([PAST_LEN - n_cache + np.arange(n_cache), PAST_LEN + np.arange(n_new)])
    cq, ck = q_pos[:, None] // CHUNK, k_pos[None, :] // CHUNK
    allowed = (ck <= cq) & (cq - ck <= N_LEFT_CHUNKS)
    bias = jnp.concatenate([_rel_bias_table(rel_bias, n_new, n_cache, n_cache),
                            _rel_bias_table(rel_bias, n_new, n_new, 0)], axis=2)
    return jnp.where(allowed[None], bias, NEG_INF)


def _attn_sample(hact, cache_k, cache_v, bias, *, n_streams, n_new):
    n_cache = cache_k.shape[1] // N_HEADS
    new_spec = lambda col: pl.BlockSpec((n_new, COL), lambda b: (b, col))
    cache_spec = pl.BlockSpec((1, n_cache * N_HEADS, HEAD_DIM), lambda b: (b, 0, 0))
    return pl.pallas_call(
        _attn_sample_kernel,
        grid=(n_streams,),
        in_specs=[new_spec(COL_Q), new_spec(COL_K), new_spec(COL_V), cache_spec, cache_spec,
                  pl.BlockSpec((N_HEADS, n_new, n_cache + n_new), lambda b: (0, 0, 0))],
        out_specs=pl.BlockSpec((n_new, D_ATTN), lambda b: (b, 0)),
        out_shape=jax.ShapeDtypeStruct((n_streams * n_new, D_ATTN), BF16),
        compiler_params=pltpu.CompilerParams(
            dimension_semantics=("arbitrary",), vmem_limit_bytes=VMEM_LIMIT),
        name="attn_sample",
    )(hact, hact, hact, cache_k, cache_v, bias)


def _merge_kernel(x_ref, u_ref, vb_ref, ga0_ref, ga1_ref, gb0_ref, gb1_ref, a_ref,
                  wbd_ref, sb_ref, wa_ref, wb_ref, wo_ref, o_ref, s_ref):
    tm = x_ref.shape[0]
    for c in range(tm // SGU_SUB):
        rows = slice(c * SGU_SUB, (c + 1) * SGU_SUB)
        for g in range(N_GROUPS):
            cols = slice(g * GROUP_DIM, (g + 1) * GROUP_DIM)
            mixed = _dot(wbd_ref[g], vb_ref[rows, cols]) + sb_ref[:, g:g + 1]
            s_ref[rows, cols] = (u_ref[rows, cols].astype(F32) * mixed).astype(BF16)
    pa = _dot(a_ref[...], wa_ref[...])
    pb = _dot(s_ref[...], wb_ref[...])
    half = D_MODEL // 2
    m0 = ga0_ref[...].astype(F32) * pa[:, :half] + gb0_ref[...].astype(F32) * pb[:, :half]
    m1 = ga1_ref[...].astype(F32) * pa[:, half:] + gb1_ref[...].astype(F32) * pb[:, half:]
    m = jnp.concatenate([m0, m1], axis=-1).astype(BF16)
    o_ref[...] = x_ref[...] + _dot(m, wo_ref[...])


def _sgu_block_weights(w_s, b_s, chunk):
    reps = SGU_SUB // chunk
    pos = np.arange(SGU_SUB)
    place = jnp.asarray(pos[:, None] % chunk == np.arange(chunk)[None, :], BF16)
    same_block = pos[:, None] // chunk == pos[None, :] // chunk
    w = (w_s * np.tril(np.ones((SGU_CHUNK, SGU_CHUNK), np.float32)))[:, :chunk, :chunk].astype(BF16)
    tiled = jnp.einsum('rc,gcd,sd->grs', place, w, place, preferred_element_type=F32)
    wbd = jnp.where(same_block[None], tiled, 0.0).astype(BF16)
    bias = jnp.tile(b_s[:, :chunk].T, (reps, 1)).astype(F32)
    return wbd, bias


def _merge(x, hact, a, wbd, sbias, wa, wb, wo, *, tm):
    m = x.shape[0]
    hcol = lambda col: pl.BlockSpec((tm, COL), lambda i: (i, col))
    const = lambda shape: pl.BlockSpec(shape, lambda i: (0,) * len(shape), pipeline_mode=pl.Buffered(1))
    return pl.pallas_call(
        _merge_kernel,
        grid=(m // tm,),
        in_specs=[pl.BlockSpec((tm, D_MODEL), lambda i: (i, 0)),
                  hcol(COL_U), hcol(COL_VB), hcol(COL_GA), hcol(COL_GA + 1), hcol(COL_GB), hcol(COL_GB + 1),
                  pl.BlockSpec((tm, D_ATTN), lambda i: (i, 0)),
                  const((N_GROUPS, SGU_SUB, SGU_SUB)), const((SGU_SUB, N_GROUPS)),
                  const((D_ATTN, D_MODEL)), const((D_SGU, D_MODEL)), const((D_MODEL, D_MODEL))],
        out_specs=pl.BlockSpec((tm, D_MODEL), lambda i: (i, 0)),
        out_shape=jax.ShapeDtypeStruct((m, D_MODEL), F32),
        scratch_shapes=[pltpu.VMEM((tm, D_SGU), BF16)],
        compiler_params=pltpu.CompilerParams(
            dimension_semantics=("arbitrary",), vmem_limit_bytes=VMEM_LIMIT),
        name="merge",
    )(x, hact, hact, hact, hact, hact, hact, a, wbd, sbias, wa, wb, wo)


def _conv_gate(hg, hv, prev_g, prev_v, cwg, cwv, cbg, cbv):
    def conv(h, prev, cw, cb):
        return cb + cw[0:1, :] * prev(2) + cw[1:2, :] * prev(1) + cw[2:3, :] * h
    return (_gelu(conv(hg, prev_g, cwg, cbg)) * conv(hv, prev_v, cwv, cbv)).astype(BF16)


def _ffn_prologue(j, x_ref, g_ref, y_ref, xn_ref):
    @pl.when(j == 0)
    def _():
        x = x_ref[...]
        xn_ref[...] = _rms(x, g_ref[...]).astype(BF16)
        y_ref[...] = x


def _ffn_epilogue(j, gf_ref, y_ref):
    @pl.when(j == N_FF_TILES - 1)
    def _():
        y_ref[...] = _rms(y_ref[...], gf_ref[...])


def _ffn_prompt_kernel(x_ref, g_ref, wg_ref, wv_ref, cw_ref, cb_ref, wd_ref, gf_ref,
                       y_ref, hl_ref, xn_ref, cg_ref, cv_ref, *, blocks_per_seq):
    i = pl.program_id(0)
    j = pl.program_id(1)
    tm = x_ref.shape[0]

    @pl.when(i % blocks_per_seq == 0)
    def _():
        cg_ref[j] = jnp.zeros(cg_ref.shape[1:], F32)
        cv_ref[j] = jnp.zeros(cv_ref.shape[1:], F32)

    def delayed(h, carry):
        def prev(k):
            head = jnp.concatenate([carry, h[:CARRY_ROWS]], axis=0)[CARRY_ROWS - k:2 * CARRY_ROWS - k]
            return jnp.concatenate([head, pltpu.roll(h, k, axis=0)[CARRY_ROWS:]], axis=0)
        return prev

    def step(first, last):
        if first:
            xn_ref[...] = _rms(x_ref[...], g_ref[...]).astype(BF16)
        up = lambda w_ref: jnp.concatenate(
            [_dot(xn_ref[r:r + FFN_SUB, :], w_ref[...]) for r in range(0, tm, FFN_SUB)], axis=0)
        hg = up(wg_ref)
        hv = up(wv_ref)
        act = _conv_gate(hg, hv, delayed(hg, cg_ref[j]), delayed(hv, cv_ref[j]),
                         cw_ref[j], cw_ref[N_FF_TILES + j], cb_ref[j], cb_ref[N_FF_TILES + j])
        y = (x_ref if first else y_ref)[...] + _dot(act, wd_ref[...])
        y_ref[...] = _rms(y, gf_ref[...]) if last else y
        last_g, last_v = hg[tm - CARRY_ROWS:], hv[tm - CARRY_ROWS:]
        cg_ref[j] = last_g
        cv_ref[j] = last_v
        hl_ref[0, 0, j] = last_g
        hl_ref[0, 1, j] = last_v

    @pl.when(j == 0)
    def _():
        step(True, False)

    @pl.when((j > 0) & (j < N_FF_TILES - 1))
    def _():
        step(False, False)

    @pl.when(j == N_FF_TILES - 1)
    def _():
        step(False, True)


def _ffn_sample_kernel(x_ref, g_ref, wg_ref, wv_ref, cwg_ref, cwv_ref, cbg_ref, cbv_ref, wd_ref, gf_ref,
                       pg_ref, pv_ref, y_ref, hg_ref, hv_ref, xn_ref, *, seq):
    j = pl.program_id(1)
    tm = x_ref.shape[0]
    _ffn_prologue(j, x_ref, g_ref, y_ref, xn_ref)
    hg = _dot(xn_ref[...], wg_ref[...])
    hv = _dot(xn_ref[...], wv_ref[...])
    hg_ref[...] = hg
    hv_ref[...] = hv
    pos = lax.broadcasted_iota(jnp.int32, (tm, FF_TILE), 0) % seq

    def delayed(h, hist_ref):
        def expand(t):
            n = hist_ref.shape[0]
            return jnp.broadcast_to(hist_ref[:, t:t + 1, :], (n, seq, FF_TILE)).reshape(tm, FF_TILE)

        def prev(k):
            rolled = pltpu.roll(h, k, axis=0)
            if k == 1:
                return jnp.where(pos == 0, expand(1), rolled)
            return jnp.where(pos == 0, expand(0), jnp.where(pos == 1, expand(1), rolled))
        return prev

    act = _conv_gate(hg, hv, delayed(hg, pg_ref), delayed(hv, pv_ref),
                     cwg_ref[...], cwv_ref[...], cbg_ref[...], cbv_ref[...])
    y_ref[...] += _dot(act, wd_ref[...])
    _ffn_epilogue(j, gf_ref, y_ref)


def _ffn_common_specs(tm):
    row = lambda i, j: (i, 0)
    fixed = lambda i, j: (0, 0)
    gate = lambda i, j: (0, j)
    val = lambda i, j: (0, N_FF_TILES + j)
    return [
        pl.BlockSpec((tm, D_MODEL), row),
        pl.BlockSpec((1, D_MODEL), fixed),
        pl.BlockSpec((D_MODEL, FF_TILE), gate),
        pl.BlockSpec((D_MODEL, FF_TILE), val),
        pl.BlockSpec((CONV_W, FF_TILE), gate),
        pl.BlockSpec((CONV_W, FF_TILE), val),
        pl.BlockSpec((1, FF_TILE), gate),
        pl.BlockSpec((1, FF_TILE), val),
        pl.BlockSpec((FF_TILE, D_MODEL), lambda i, j: (j, 0)),
        pl.BlockSpec((1, D_MODEL), fixed),
    ]


def _ffn_prompt(x, g, w_up, conv_w, conv_b, w_down, gf, *, tm, rows_per_seq):
    m = x.shape[0]
    blocks_per_seq = rows_per_seq // tm
    n_tiles = 2 * N_FF_TILES
    cw_tiles = conv_w.reshape(CONV_W, n_tiles, FF_TILE).transpose(1, 0, 2)
    cb_tiles = conv_b.reshape(n_tiles, 1, FF_TILE)
    row = lambda i, j: (i, 0)
    fixed2 = lambda i, j: (0, 0)
    fixed3 = lambda i, j: (0, 0, 0)
    last_shape = (1, 2, N_FF_TILES, CARRY_ROWS, FF_TILE)
    carry = pltpu.VMEM((N_FF_TILES, CARRY_ROWS, FF_TILE), F32)
    y, h_last = pl.pallas_call(
        functools.partial(_ffn_prompt_kernel, blocks_per_seq=blocks_per_seq),
        grid=(m // tm, N_FF_TILES),
        in_specs=[
            pl.BlockSpec((tm, D_MODEL), row),
            pl.BlockSpec((1, D_MODEL), fixed2),
            pl.BlockSpec((D_MODEL, FF_TILE), lambda i, j: (0, j)),
            pl.BlockSpec((D_MODEL, FF_TILE), lambda i, j: (0, N_FF_TILES + j)),
            pl.BlockSpec((n_tiles, CONV_W, FF_TILE), fixed3),
            pl.BlockSpec((n_tiles, 1, FF_TILE), fixed3),
            pl.BlockSpec((FF_TILE, D_MODEL), lambda i, j: (j, 0)),
            pl.BlockSpec((1, D_MODEL), fixed2),
        ],
        out_specs=[pl.BlockSpec((tm, D_MODEL), row), pl.BlockSpec(last_shape, lambda i, j: (i, 0, 0, 0, 0))],
        out_shape=[jax.ShapeDtypeStruct((m, D_MODEL), F32),
                   jax.ShapeDtypeStruct((m // tm,) + last_shape[1:], F32)],
        scratch_shapes=[pltpu.VMEM((tm, D_MODEL), BF16), carry, carry],
        compiler_params=pltpu.CompilerParams(
            dimension_semantics=("arbitrary", "arbitrary"), vmem_limit_bytes=VMEM_LIMIT_FFN),
        name="ffn_prompt",
    )(x, g, w_up, w_up, cw_tiles, cb_tiles, w_down, gf)
    tail = h_last[blocks_per_seq - 1::blocks_per_seq, :, :, CARRY_ROWS - (CONV_W - 1):, :]
    return y, tail.transpose(0, 3, 1, 2, 4).reshape(tail.shape[0], CONV_W - 1, 2 * D_FF)


def _ffn_sample(x, g, w_up, conv_w, conv_b, w_down, gf, hist, *, seq):
    m = x.shape[0]
    n_streams = m // seq
    hist_g = pl.BlockSpec((n_streams, CONV_W - 1, FF_TILE), lambda i, j: (0, 0, j))
    hist_v = pl.BlockSpec((n_streams, CONV_W - 1, FF_TILE), lambda i, j: (0, 0, N_FF_TILES + j))
    h_spec = pl.BlockSpec((m, FF_TILE), lambda i, j: (0, j))
    h_shape = jax.ShapeDtypeStruct((m, D_FF), F32)
    return pl.pallas_call(
        functools.partial(_ffn_sample_kernel, seq=seq),
        grid=(1, N_FF_TILES),
        in_specs=_ffn_common_specs(m) + [hist_g, hist_v],
        out_specs=[pl.BlockSpec((m, D_MODEL), lambda i, j: (0, 0)), h_spec, h_spec],
        out_shape=[jax.ShapeDtypeStruct((m, D_MODEL), F32), h_shape, h_shape],
        scratch_shapes=[pltpu.VMEM((m, D_MODEL), BF16)],
        compiler_params=pltpu.CompilerParams(
            dimension_semantics=("arbitrary", "arbitrary"), vmem_limit_bytes=VMEM_LIMIT),
        name="ffn_sample",
    )(x, g, w_up, w_up, conv_w, conv_w, conv_b, conv_b, w_down, gf, hist, hist)


def kernel(x_prompt, x_sample, cache_k, cache_v, cache_ffn_conv, norm_mix_g, w_in, rel_bias, sgu_norm_g, w_s, b_s,
           w_branch_a, w_branch_b, w_out, norm_ffn_g, w_up, conv_w, conv_b, w_down, norm_final_g):
    depth = w_in.shape[0]
    assert depth == 1, "single-layer trunk"
    batch, seq, _ = x_prompt.shape
    n_streams, n_new, _ = x_sample.shape
    n_cache = cache_k.shape[2]
    keep = min(KV_WINDOW, seq)

    row = lambda v: v.reshape(1, -1).astype(F32)
    g_mix, g_sgu, g_ffn, g_fin = row(norm_mix_g[0]), row(sgu_norm_g[0]), row(norm_ffn_g[0]), row(norm_final_g)
    cw, cb = conv_w[0].astype(F32), row(conv_b[0])

    ms = n_streams * n_new
    xs = x_sample.reshape(ms, D_MODEL)
    hact_s, k_new, v_new, vb_new, w_in_bf = _in_proj(xs, g_mix, w_in[0], g_sgu, tm=ms, rows_per_seq=ms, tail=ms,
                                                     vb_tail=True)

    xp = x_prompt.reshape(batch * seq, D_MODEL)
    hact_p, k_tail, v_tail, wa_bf, wb_bf, wo_bf, w_up_bf, w_down_bf = _in_proj(
        xp, g_mix, w_in_bf, g_sgu, tm=min(PROJ_ROWS, seq), rows_per_seq=seq, tail=keep, vb_tail=False,
        cast=(w_branch_a[0], w_branch_b[0], w_out[0], w_up[0], w_down[0]))
    a_p = _attn_prompt(hact_p, rel_bias[0], batch=batch, seq=seq)
    wbd_p, sb_p = _sgu_block_weights(w_s[0], b_s[0], SGU_CHUNK)
    x1_p = _merge(xp, hact_p, a_p, wbd_p, sb_p, wa_bf, wb_bf, wo_bf, tm=MERGE_ROWS)
    ffn_rows = min(FFN_ROWS, seq)
    y_p, conv_p = _ffn_prompt(x1_p, g_ffn, w_up_bf, cw, cb, w_down_bf, g_fin, tm=ffn_rows, rows_per_seq=seq)

    a_s = _attn_sample(hact_s, cache_k[0].reshape(n_streams, n_cache * N_HEADS, HEAD_DIM),
                       cache_v[0].reshape(n_streams, n_cache * N_HEADS, HEAD_DIM),
                       _sample_bias(rel_bias[0], n_cache, n_new), n_streams=n_streams, n_new=n_new)
    wbd_s, sb_s = _sgu_block_weights(w_s[0], b_s[0], n_new)
    x1_s = _merge(xs, hact_s, a_s, wbd_s, sb_s, wa_bf, wb_bf, wo_bf, tm=ms)
    y_s, h_g, h_v = _ffn_sample(x1_s, g_ffn, w_up_bf, cw, cb, w_down_bf, g_fin,
                                cache_ffn_conv[0], seq=n_new)

    hist = CONV_W - 1
    heads = lambda t, b, s: t.reshape(1, b, s, N_HEADS, HEAD_DIM)
    new_conv_prompt = conv_p[None]
    h_s = jnp.concatenate([h_g, h_v], axis=-1).reshape(n_streams, n_new, 2 * D_FF)
    new_conv_sample = h_s[:, n_new - hist:][None]
    return (
        y_p.reshape(batch, seq, D_MODEL),
        y_s.reshape(n_streams, n_new, D_MODEL),
        heads(k_tail, batch, keep),
        heads(v_tail, batch, keep),
        heads(k_new, n_streams, n_new),
        heads(v_new, n_streams, n_new),
        vb_new.reshape(1, n_streams, n_new, D_SGU),
        new_conv_prompt,
        new_conv_sample,
    )
```

```python
import functools

import numpy as np
import jax
import jax.numpy as jnp
from jax import lax
from jax.experimental import pallas as pl
from jax.experimental.pallas import tpu as pltpu

D_MODEL = 2048
CHUNK = 64
N_LEFT_CHUNKS = 8
KV_WINDOW = N_LEFT_CHUNKS * CHUNK
D_ATTN = D_MODEL // 2
N_HEADS = 8
HEAD_DIM = D_ATTN // N_HEADS
MAX_REL = 256
D_SGU = D_MODEL // 2
N_GROUPS = 8
GROUP_DIM = D_SGU // N_GROUPS
SGU_CHUNK = 128
D_FF = 5632
CONV_W = 3
EPS = 1e-6
PAST_LEN = 2048
D_IN = 3 * D_ATTN + 2 * D_SGU + 2 * D_MODEL
NEG_INF = -1e30

COL = 1024
COL_Q, COL_K, COL_V, COL_U, COL_VB, COL_GA, COL_GB = 0, 1, 2, 3, 4, 5, 7
N_COL_BLOCKS = D_IN // COL

PROJ_ROWS = 1024
MERGE_ROWS = 512
FFN_ROWS = 1024
FFN_SUB = 256
PROJ_COL_CHUNK = 512
PROJ_ROW_CHUNK = 256
ATTN_QB = 256
LOG2E = float(np.log2(np.e))
SGU_SUB = 256
FF_TILE = 512
N_FF_TILES = D_FF // FF_TILE
CARRY_ROWS = 8
BF16_SUBLANES = 16

VMEM_LIMIT = 56 * 1024 * 1024
VMEM_LIMIT_FFN = 60 * 1024 * 1024

BF16 = jnp.bfloat16
F32 = jnp.float32


def _rms(x, g):
    inv = lax.rsqrt(jnp.mean(x * x, axis=-1, keepdims=True) + EPS)
    return (x * inv) * g


def _gelu(x):
    return 0.5 * x * (1.0 + lax.erf(x * (2.0 ** -0.5)))


def _sigmoid(x):
    return 0.5 * jnp.tanh(0.5 * x) + 0.5


def _dot(a, b):
    return jnp.dot(a, b, preferred_element_type=F32)


def _dot_nt(a, b):
    return lax.dot_general(a, b, (((1,), (1,)), ((), ())), preferred_element_type=F32)


def _in_proj_kernel(x_ref, g_ref, w_ref, sg_ref, *refs, tail, vb_tail, n_cast):
    cast_in, refs = refs[:n_cast], refs[n_cast:]
    h_ref, kt_ref, vt_ref = refs[:3]
    vbt_ref = refs[3] if vb_tail else None
    cast_out, refs = refs[3 + vb_tail:3 + vb_tail + n_cast], refs[3 + vb_tail + n_cast:]
    wbf_ref = refs[0] if len(refs) == 2 else None
    xn_ref = refs[-1]
    j = pl.program_id(1)
    tm = x_ref.shape[0]
    head = tm - tail

    @pl.when(j == 0)
    def _():
        xn_ref[...] = _rms(x_ref[...], g_ref[...]).astype(BF16)

    def side_cast():
        for src, dst in zip(cast_in, cast_out):
            dst[...] = src[...].astype(BF16)

    def weights():
        if wbf_ref is None:
            return w_ref
        wbf_ref[...] = w_ref[...].astype(BF16)
        return wbf_ref

    def by_cols(act, tail_ref=None):
        side_cast()
        w_ref = weights()
        for c in range(COL // PROJ_COL_CHUNK):
            cols = slice(c * PROJ_COL_CHUNK, (c + 1) * PROJ_COL_CHUNK)
            acc = _dot(xn_ref[...], w_ref[:, cols])
            h_ref[:, cols] = act(acc).astype(BF16)
            if tail_ref is not None:
                tail_ref[:, cols] = acc[head:, :]

    @pl.when(j == COL_Q)
    def _():
        by_cols(lambda a: a)

    @pl.when(j == COL_K)
    def _():
        by_cols(lambda a: a, kt_ref)

    @pl.when(j == COL_V)
    def _():
        by_cols(lambda a: a, vt_ref)

    @pl.when(j == COL_U)
    def _():
        by_cols(_gelu)

    @pl.when(j == COL_VB)
    def _():
        side_cast()
        w_bf = weights()
        for r in range(tm // PROJ_ROW_CHUNK):
            lo = r * PROJ_ROW_CHUNK
            rows = slice(lo, lo + PROJ_ROW_CHUNK)
            vb = _rms(_gelu(_dot(xn_ref[rows, :], w_bf[...])), sg_ref[...])
            h_ref[rows, :] = vb.astype(BF16)
            if vbt_ref is not None and lo >= head:
                vbt_ref[lo - head:lo - head + PROJ_ROW_CHUNK, :] = vb

    @pl.when(j >= COL_GA)
    def _():
        by_cols(_sigmoid)


def _in_proj(x, g, w, sg, *, tm, rows_per_seq, tail, vb_tail, cast=()):
    m = x.shape[0]
    emit_w = w.dtype != BF16
    assert not emit_w or m == tm, "the weight copy is written once, so it needs a single row block"
    blocks_per_seq = rows_per_seq // tm
    n_seq = m // rows_per_seq
    assert tail <= tm and (tm - tail) % PROJ_ROW_CHUNK == 0
    n_tails = 3 if vb_tail else 2
    n_steps = (m // tm) * N_COL_BLOCKS
    tail_spec = pl.BlockSpec((tail, COL), lambda i, j: (i // blocks_per_seq, 0))
    tail_shape = jax.ShapeDtypeStruct((n_seq * tail, COL), F32)

    def slab_spec(w):
        rows = BF16_SUBLANES
        while w.shape[0] // rows > n_steps:
            rows *= 2
        assert w.shape[0] % rows == 0
        last = w.shape[0] // rows - 1
        return pl.BlockSpec((rows, w.shape[1]), lambda i, j: (jnp.minimum(i * N_COL_BLOCKS + j, last), 0))

    slab_specs = [slab_spec(c) for c in cast]
    w_spec = pl.BlockSpec((D_MODEL, COL), lambda i, j: (0, j))
    return pl.pallas_call(
        functools.partial(_in_proj_kernel, tail=tail, vb_tail=vb_tail, n_cast=len(cast)),
        grid=(m // tm, N_COL_BLOCKS),
        in_specs=[
            pl.BlockSpec((tm, D_MODEL), lambda i, j: (i, 0)),
            pl.BlockSpec((1, D_MODEL), lambda i, j: (0, 0)),
            w_spec,
            pl.BlockSpec((1, D_SGU), lambda i, j: (0, 0)),
        ] + slab_specs,
        out_specs=[pl.BlockSpec((tm, COL), lambda i, j: (i, j))] + [tail_spec] * n_tails + slab_specs
        + [w_spec] * emit_w,
        out_shape=[jax.ShapeDtypeStruct((m, D_IN), BF16)] + [tail_shape] * n_tails
        + [jax.ShapeDtypeStruct(c.shape, BF16) for c in cast] + [jax.ShapeDtypeStruct(w.shape, BF16)] * emit_w,
        scratch_shapes=[pltpu.VMEM((tm, D_MODEL), BF16)],
        compiler_params=pltpu.CompilerParams(
            dimension_semantics=("arbitrary", "arbitrary"), vmem_limit_bytes=VMEM_LIMIT),
        name="in_proj",
    )(x, g, w, sg, *cast)


def _attn_prompt_kernel(q_ref, k0_ref, k1_ref, k2_ref, v0_ref, v1_ref, v2_ref, diag_ref, o_ref, bias_ref):
    qb = pl.program_id(1)
    k_refs = (k0_ref, k1_ref, k2_ref)
    v_refs = (v0_ref, v1_ref, v2_ref)
    scale2 = HEAD_DIM ** -0.5 * LOG2E
    n_keys = 3 * ATTN_QB

    @pl.when((pl.program_id(0) == 0) & (qb == 0))
    def _():
        qi = lax.broadcasted_iota(jnp.int32, (ATTN_QB, n_keys), 0) // CHUNK
        ck = lax.broadcasted_iota(jnp.int32, (ATTN_QB, n_keys), 1) // CHUNK - (2 * ATTN_QB) // CHUNK
        allowed = (ck <= qi) & (qi - ck <= N_LEFT_CHUNKS)
        width = diag_ref.shape[1]
        for h in range(N_HEADS):
            row = jnp.broadcast_to(diag_ref[h:h + 1, :], (ATTN_QB, width))
            skew = pltpu.roll(row, width - (ATTN_QB - 1), 1, stride=1, stride_axis=0)
            bias_ref[h] = jnp.where(allowed, skew[:, :n_keys] * LOG2E, NEG_INF)

    ones = jnp.ones((3 * ATTN_QB, HEAD_DIM), BF16)

    def attend(mask_missing):
        for h in range(N_HEADS):
            cols = slice(h * HEAD_DIM, (h + 1) * HEAD_DIM)
            k = jnp.concatenate([r[:, cols] for r in k_refs], axis=0)
            v = jnp.concatenate([jnp.concatenate([r[:, cols] for r in v_refs], axis=0), ones], axis=1)
            s = _dot_nt(q_ref[:, cols], k) * scale2 + bias_ref[h]
            if mask_missing:
                s = jnp.concatenate(
                    [s[:, r * ATTN_QB:(r + 1) * ATTN_QB] + jnp.where(qb - 2 + r >= 0, 0.0, NEG_INF).astype(F32)
                     for r in range(3)], axis=1)
            e = jnp.exp2(s - s.max(-1, keepdims=True)).astype(BF16)
            o = _dot(e, v)
            o_ref[:, cols] = (o[:, :HEAD_DIM] / o[:, HEAD_DIM:]).astype(BF16)

    @pl.when(qb < KV_WINDOW // ATTN_QB)
    def _():
        attend(True)

    @pl.when(qb >= KV_WINDOW // ATTN_QB)
    def _():
        attend(False)


def _rel_bias_diagonals(rel_bias, n, m, d0):
    dist = np.arange(n + m - 1) - (m - 1) + d0
    diag = rel_bias[:, np.clip(dist, -MAX_REL, MAX_REL) + MAX_REL].astype(F32)
    rev = diag[:, ::-1]
    return jnp.concatenate([rev, rev[:, :1]], axis=1)


def _rel_bias_table(rel_bias, n, m, d0):
    length = n + m - 1
    padded = _rel_bias_diagonals(rel_bias, n, m, d0)
    skew = jnp.tile(padded, (1, n))[:, :n * length].reshape(-1, n, length)
    return skew[:, :, n - 1:n - 1 + m]


def _attn_prompt(hact, rel_bias, *, batch, seq):
    nqb = seq // ATTN_QB
    n_keys = 3 * ATTN_QB
    diag = _rel_bias_diagonals(rel_bias, ATTN_QB, n_keys, 2 * ATTN_QB)
    kv_spec = lambda col, r: pl.BlockSpec(
        (ATTN_QB, COL), lambda b, t: (b * nqb + jnp.maximum(t - 2 + r, 0), col))
    return pl.pallas_call(
        _attn_prompt_kernel,
        grid=(batch, nqb),
        in_specs=[pl.BlockSpec((ATTN_QB, COL), lambda b, t: (b * nqb + t, COL_Q))]
        + [kv_spec(COL_K, r) for r in range(3)] + [kv_spec(COL_V, r) for r in range(3)]
        + [pl.BlockSpec(diag.shape, lambda b, t: (0, 0))],
        out_specs=pl.BlockSpec((ATTN_QB, D_ATTN), lambda b, t: (b * nqb + t, 0)),
        out_shape=jax.ShapeDtypeStruct((batch * seq, D_ATTN), BF16),
        scratch_shapes=[pltpu.VMEM((N_HEADS, ATTN_QB, n_keys), F32)],
        compiler_params=pltpu.CompilerParams(
            dimension_semantics=("arbitrary", "arbitrary"), vmem_limit_bytes=VMEM_LIMIT),
        name="attn_prompt",
    )(hact, hact, hact, hact, hact, hact, hact, diag)


def _attn_sample_kernel(q_ref, kn_ref, vn_ref, kc_ref, vc_ref, bias_ref, o_ref):
    n_cache = kc_ref.shape[1] // N_HEADS
    scale = HEAD_DIM ** -0.5
    for h in range(N_HEADS):
        cols = slice(h * HEAD_DIM, (h + 1) * HEAD_DIM)
        head_rows = pl.ds(h, n_cache, stride=N_HEADS)
        q = q_ref[:, cols]
        s_c = _dot_nt(q, kc_ref[0, head_rows, :].astype(BF16)) * scale + bias_ref[h, :, :n_cache]
        s_n = _dot_nt(q, kn_ref[:, cols]) * scale + bias_ref[h, :, n_cache:]
        mx = jnp.maximum(s_c.max(-1, keepdims=True), s_n.max(-1, keepdims=True))
        e_c, e_n = jnp.exp(s_c - mx), jnp.exp(s_n - mx)
        den = e_c.sum(-1, keepdims=True) + e_n.sum(-1, keepdims=True)
        o = (_dot(e_c.astype(BF16), vc_ref[0, head_rows, :].astype(BF16))
             + _dot(e_n.astype(BF16), vn_ref[:, cols]))
        o_ref[:, cols] = (o / den).astype(BF16)


def _sample_bias(rel_bias, n_cache, n_new):
    q_pos = PAST_LEN + np.arange(n_new)
    k_pos = np.concatenate([PAST_LEN - n_cache + np.arange(n_cache), PAST_LEN + np.arange(n_new)])
    cq, ck = q_pos[:, None] // CHUNK, k_pos[None, :] // CHUNK
    allowed = (ck <= cq) & (cq - ck <= N_LEFT_CHUNKS)
    bias = jnp.concatenate([_rel_bias_table(rel_bias, n_new, n_cache, n_cache),
                            _rel_bias_table(rel_bias, n_new, n_new, 0)], axis=2)
    return jnp.where(allowed[None], bias, NEG_INF)


def _attn_sample(hact, cache_k, cache_v, bias, *, n_streams, n_new):
    n_cache = cache_k.shape[1] // N_HEADS
    new_spec = lambda col: pl.BlockSpec((n_new, COL), lambda b: (b, col))
    cache_spec = pl.BlockSpec((1, n_cache * N_HEADS, HEAD_DIM), lambda b: (b, 0, 0))
    return pl.pallas_call(
        _attn_sample_kernel,
        grid=(n_streams,),
        in_specs=[new_spec(COL_Q), new_spec(COL_K), new_spec(COL_V), cache_spec, cache_spec,
                  pl.BlockSpec((N_HEADS, n_new, n_cache + n_new), lambda b: (0, 0, 0))],
        out_specs=pl.BlockSpec((n_new, D_ATTN), lambda b: (b, 0)),
        out_shape=jax.ShapeDtypeStruct((n_streams * n_new, D_ATTN), BF16),
        compiler_params=pltpu.CompilerParams(
            dimension_semantics=("arbitrary",), vmem_limit_bytes=VMEM_LIMIT),
        name="attn_sample",
    )(hact, hact, hact, cache_k, cache_v, bias)


def _merge_kernel(x_ref, u_ref, vb_ref, ga0_ref, ga1_ref, gb0_ref, gb1_ref, a_ref,
                  wbd_ref, sb_ref, wa_ref, wb_ref, wo_ref, o_ref, s_ref):
    tm = x_ref.shape[0]
    for c in range(tm // SGU_SUB):
        rows = slice(c * SGU_SUB, (c + 1) * SGU_SUB)
        for g in range(N_GROUPS):
            cols = slice(g * GROUP_DIM, (g + 1) * GROUP_DIM)
            mixed = _dot(wbd_ref[g], vb_ref[rows, cols]) + sb_ref[:, g:g + 1]
            s_ref[rows, cols] = (u_ref[rows, cols].astype(F32) * mixed).astype(BF16)
    pa = _dot(a_ref[...], wa_ref[...])
    pb = _dot(s_ref[...], wb_ref[...])
    half = D_MODEL // 2
    m0 = ga0_ref[...].astype(F32) * pa[:, :half] + gb0_ref[...].astype(F32) * pb[:, :half]
    m1 = ga1_ref[...].astype(F32) * pa[:, half:] + gb1_ref[...].astype(F32) * pb[:, half:]
    m = jnp.concatenate([m0, m1], axis=-1).astype(BF16)
    o_ref[...] = x_ref[...] + _dot(m, wo_ref[...])


def _sgu_block_weights(w_s, b_s, chunk):
    reps = SGU_SUB // chunk
    pos = np.arange(SGU_SUB)
    place = jnp.asarray(pos[:, None] % chunk == np.arange(chunk)[None, :], BF16)
    same_block = pos[:, None] // chunk == pos[None, :] // chunk
    w = (w_s * np.tril(np.ones((SGU_CHUNK, SGU_CHUNK), np.float32)))[:, :chunk, :chunk].astype(BF16)
    tiled = jnp.einsum('rc,gcd,sd->grs', place, w, place, preferred_element_type=F32)
    wbd = jnp.where(same_block[None], tiled, 0.0).astype(BF16)
    bias = jnp.tile(b_s[:, :chunk].T, (reps, 1)).astype(F32)
    return wbd, bias


def _merge(x, hact, a, wbd, sbias, wa, wb, wo, *, tm):
    m = x.shape[0]
    hcol = lambda col: pl.BlockSpec((tm, COL), lambda i: (i, col))
    const = lambda shape: pl.BlockSpec(shape, lambda i: (0,) * len(shape), pipeline_mode=pl.Buffered(1))
    return pl.pallas_call(
        _merge_kernel,
        grid=(m // tm,),
        in_specs=[pl.BlockSpec((tm, D_MODEL), lambda i: (i, 0)),
                  hcol(COL_U), hcol(COL_VB), hcol(COL_GA), hcol(COL_GA + 1), hcol(COL_GB), hcol(COL_GB + 1),
                  pl.BlockSpec((tm, D_ATTN), lambda i: (i, 0)),
                  const((N_GROUPS, SGU_SUB, SGU_SUB)), const((SGU_SUB, N_GROUPS)),
                  const((D_ATTN, D_MODEL)), const((D_SGU, D_MODEL)), const((D_MODEL, D_MODEL))],
        out_specs=pl.BlockSpec((tm, D_MODEL), lambda i: (i, 0)),
        out_shape=jax.ShapeDtypeStruct((m, D_MODEL), F32),
        scratch_shapes=[pltpu.VMEM((tm, D_SGU), BF16)],
        compiler_params=pltpu.CompilerParams(
            dimension_semantics=("arbitrary",), vmem_limit_bytes=VMEM_LIMIT),
        name="merge",
    )(x, hact, hact, hact, hact, hact, hact, a, wbd, sbias, wa, wb, wo)


def _conv_gate(hg, hv, prev_g, prev_v, cwg, cwv, cbg, cbv):
    def conv(h, prev, cw, cb):
        return cb + cw[0:1, :] * prev(2) + cw[1:2, :] * prev(1) + cw[2:3, :] * h
    return (_gelu(conv(hg, prev_g, cwg, cbg)) * conv(hv, prev_v, cwv, cbv)).astype(BF16)


def _ffn_prologue(j, x_ref, g_ref, y_ref, xn_ref):
    @pl.when(j == 0)
    def _():
        x = x_ref[...]
        xn_ref[...] = _rms(x, g_ref[...]).astype(BF16)
        y_ref[...] = x


def _ffn_epilogue(j, gf_ref, y_ref):
    @pl.when(j == N_FF_TILES - 1)
    def _():
        y_ref[...] = _rms(y_ref[...], gf_ref[...])


def _ffn_prompt_kernel(x_ref, g_ref, wg_ref, wv_ref, cw_ref, cb_ref, wd_ref, gf_ref,
                       y_ref, hl_ref, xn_ref, cg_ref, cv_ref, *, blocks_per_seq):
    i = pl.program_id(0)
    j = pl.program_id(1)
    tm = x_ref.shape[0]

    @pl.when(i % blocks_per_seq == 0)
    def _():
        cg_ref[j] = jnp.zeros(cg_ref.shape[1:], F32)
        cv_ref[j] = jnp.zeros(cv_ref.shape[1:], F32)

    def delayed(h, carry):
        def prev(k):
            head = jnp.concatenate([carry, h[:CARRY_ROWS]], axis=0)[CARRY_ROWS - k:2 * CARRY_ROWS - k]
            return jnp.concatenate([head, pltpu.roll(h, k, axis=0)[CARRY_ROWS:]], axis=0)
        return prev

    def step(first, last):
        if first:
            xn_ref[...] = _rms(x_ref[...], g_ref[...]).astype(BF16)
        up = lambda w_ref: jnp.concatenate(
            [_dot(xn_ref[r:r + FFN_SUB, :], w_ref[...]) for r in range(0, tm, FFN_SUB)], axis=0)
        hg = up(wg_ref)
        hv = up(wv_ref)
        act = _conv_gate(hg, hv, delayed(hg, cg_ref[j]), delayed(hv, cv_ref[j]),
                         cw_ref[j], cw_ref[N_FF_TILES + j], cb_ref[j], cb_ref[N_FF_TILES + j])
        y = (x_ref if first else y_ref)[...] + _dot(act, wd_ref[...])
        y_ref[...] = _rms(y, gf_ref[...]) if last else y
        last_g, last_v = hg[tm - CARRY_ROWS:], hv[tm - CARRY_ROWS:]
        cg_ref[j] = last_g
        cv_ref[j] = last_v
        hl_ref[0, 0, j] = last_g
        hl_ref[0, 1, j] = last_v

    @pl.when(j == 0)
    def _():
        step(True, False)

    @pl.when((j > 0) & (j < N_FF_TILES - 1))
    def _():
        step(False, False)

    @pl.when(j == N_FF_TILES - 1)
    def _():
        step(False, True)


def _ffn_sample_kernel(x_ref, g_ref, wg_ref, wv_ref, cwg_ref, cwv_ref, cbg_ref, cbv_ref, wd_ref, gf_ref,
                       pg_ref, pv_ref, y_ref, hg_ref, hv_ref, xn_ref, *, seq):
    j = pl.program_id(1)
    tm = x_ref.shape[0]
    _ffn_prologue(j, x_ref, g_ref, y_ref, xn_ref)
    hg = _dot(xn_ref[...], wg_ref[...])
    hv = _dot(xn_ref[...], wv_ref[...])
    keep = lambda h: h.reshape(tm // seq, seq, FF_TILE)[:, seq - (CONV_W - 1):, :]
    hg_ref[...] = keep(hg)
    hv_ref[...] = keep(hv)
    pos = lax.broadcasted_iota(jnp.int32, (tm, FF_TILE), 0) % seq

    def delayed(h, hist_ref):
        def expand(t):
            n = hist_ref.shape[0]
            return jnp.broadcast_to(hist_ref[:, t:t + 1, :], (n, seq, FF_TILE)).reshape(tm, FF_TILE)

        def prev(k):
            rolled = pltpu.roll(h, k, axis=0)
            if k == 1:
                return jnp.where(pos == 0, expand(1), rolled)
            return jnp.where(pos == 0, expand(0), jnp.where(pos == 1, expand(1), rolled))
        return prev

    act = _conv_gate(hg, hv, delayed(hg, pg_ref), delayed(hv, pv_ref),
                     cwg_ref[...], cwv_ref[...], cbg_ref[...], cbv_ref[...])
    y_ref[...] += _dot(act, wd_ref[...])
    _ffn_epilogue(j, gf_ref, y_ref)


def _ffn_common_specs(tm):
    row = lambda i, j: (i, 0)
    fixed = lambda i, j: (0, 0)
    gate = lambda i, j: (0, j)
    val = lambda i, j: (0, N_FF_TILES + j)
    return [
        pl.BlockSpec((tm, D_MODEL), row),
        pl.BlockSpec((1, D_MODEL), fixed),
        pl.BlockSpec((D_MODEL, FF_TILE), gate),
        pl.BlockSpec((D_MODEL, FF_TILE), val),
        pl.BlockSpec((CONV_W, FF_TILE), gate),
        pl.BlockSpec((CONV_W, FF_TILE), val),
        pl.BlockSpec((1, FF_TILE), gate),
        pl.BlockSpec((1, FF_TILE), val),
        pl.BlockSpec((FF_TILE, D_MODEL), lambda i, j: (j, 0)),
        pl.BlockSpec((1, D_MODEL), fixed),
    ]


def _ffn_prompt(x, g, w_up, conv_w, conv_b, w_down, gf, *, tm, rows_per_seq):
    m = x.shape[0]
    blocks_per_seq = rows_per_seq // tm
    n_tiles = 2 * N_FF_TILES
    cw_tiles = conv_w.reshape(CONV_W, n_tiles, FF_TILE).transpose(1, 0, 2)
    cb_tiles = conv_b.reshape(n_tiles, 1, FF_TILE)
    row = lambda i, j: (i, 0)
    fixed2 = lambda i, j: (0, 0)
    fixed3 = lambda i, j: (0, 0, 0)
    last_shape = (1, 2, N_FF_TILES, CARRY_ROWS, FF_TILE)
    carry = pltpu.VMEM((N_FF_TILES, CARRY_ROWS, FF_TILE), F32)
    y, h_last = pl.pallas_call(
        functools.partial(_ffn_prompt_kernel, blocks_per_seq=blocks_per_seq),
        grid=(m // tm, N_FF_TILES),
        in_specs=[
            pl.BlockSpec((tm, D_MODEL), row),
            pl.BlockSpec((1, D_MODEL), fixed2),
            pl.BlockSpec((D_MODEL, FF_TILE), lambda i, j: (0, j)),
            pl.BlockSpec((D_MODEL, FF_TILE), lambda i, j: (0, N_FF_TILES + j)),
            pl.BlockSpec((n_tiles, CONV_W, FF_TILE), fixed3),
            pl.BlockSpec((n_tiles, 1, FF_TILE), fixed3),
            pl.BlockSpec((FF_TILE, D_MODEL), lambda i, j: (j, 0)),
            pl.BlockSpec((1, D_MODEL), fixed2),
        ],
        out_specs=[pl.BlockSpec((tm, D_MODEL), row), pl.BlockSpec(last_shape, lambda i, j: (i, 0, 0, 0, 0))],
        out_shape=[jax.ShapeDtypeStruct((m, D_MODEL), F32),
                   jax.ShapeDtypeStruct((m // tm,) + last_shape[1:], F32)],
        scratch_shapes=[pltpu.VMEM((tm, D_MODEL), BF16), carry, carry],
        compiler_params=pltpu.CompilerParams(
            dimension_semantics=("arbitrary", "arbitrary"), vmem_limit_bytes=VMEM_LIMIT_FFN),
        name="ffn_prompt",
    )(x, g, w_up, w_up, cw_tiles, cb_tiles, w_down, gf)
    tail = h_last[blocks_per_seq - 1::blocks_per_seq, :, :, CARRY_ROWS - (CONV_W - 1):, :]
    return y, tail.transpose(0, 3, 1, 2, 4).reshape(tail.shape[0], CONV_W - 1, 2 * D_FF)


def _ffn_sample(x, g, w_up, conv_w, conv_b, w_down, gf, hist, *, seq):
    m = x.shape[0]
    n_streams = m // seq
    hist_g = pl.BlockSpec((n_streams, CONV_W - 1, FF_TILE), lambda i, j: (0, 0, j))
    hist_v = pl.BlockSpec((n_streams, CONV_W - 1, FF_TILE), lambda i, j: (0, 0, N_FF_TILES + j))
    h_spec = pl.BlockSpec((n_streams, CONV_W - 1, FF_TILE), lambda i, j: (0, 0, j))
    h_shape = jax.ShapeDtypeStruct((n_streams, CONV_W - 1, D_FF), F32)
    return pl.pallas_call(
        functools.partial(_ffn_sample_kernel, seq=seq),
        grid=(1, N_FF_TILES),
        in_specs=_ffn_common_specs(m) + [hist_g, hist_v],
        out_specs=[pl.BlockSpec((m, D_MODEL), lambda i, j: (0, 0)), h_spec, h_spec],
        out_shape=[jax.ShapeDtypeStruct((m, D_MODEL), F32), h_shape, h_shape],
        scratch_shapes=[pltpu.VMEM((m, D_MODEL), BF16)],
        compiler_params=pltpu.CompilerParams(
            dimension_semantics=("arbitrary", "arbitrary"), vmem_limit_bytes=VMEM_LIMIT),
        name="ffn_sample",
    )(x, g, w_up, w_up, conv_w, conv_w, conv_b, conv_b, w_down, gf, hist, hist)


def kernel(x_prompt, x_sample, cache_k, cache_v, cache_ffn_conv, norm_mix_g, w_in, rel_bias, sgu_norm_g, w_s, b_s,
           w_branch_a, w_branch_b, w_out, norm_ffn_g, w_up, conv_w, conv_b, w_down, norm_final_g):
    depth = w_in.shape[0]
    assert depth == 1, "single-layer trunk"
    batch, seq, _ = x_prompt.shape
    n_streams, n_new, _ = x_sample.shape
    n_cache = cache_k.shape[2]
    keep = min(KV_WINDOW, seq)

    row = lambda v: v.reshape(1, -1).astype(F32)
    g_mix, g_sgu, g_ffn, g_fin = row(norm_mix_g[0]), row(sgu_norm_g[0]), row(norm_ffn_g[0]), row(norm_final_g)
    cw, cb = conv_w[0].astype(F32), row(conv_b[0])

    ms = n_streams * n_new
    xs = x_sample.reshape(ms, D_MODEL)
    hact_s, k_new, v_new, vb_new, w_in_bf = _in_proj(xs, g_mix, w_in[0], g_sgu, tm=ms, rows_per_seq=ms, tail=ms,
                                                     vb_tail=True)

    xp = x_prompt.reshape(batch * seq, D_MODEL)
    hact_p, k_tail, v_tail, wa_bf, wb_bf, wo_bf, w_up_bf, w_down_bf = _in_proj(
        xp, g_mix, w_in_bf, g_sgu, tm=min(PROJ_ROWS, seq), rows_per_seq=seq, tail=keep, vb_tail=False,
        cast=(w_branch_a[0], w_branch_b[0], w_out[0], w_up[0], w_down[0]))
    a_p = _attn_prompt(hact_p, rel_bias[0], batch=batch, seq=seq)
    wbd_p, sb_p = _sgu_block_weights(w_s[0], b_s[0], SGU_CHUNK)
    x1_p = _merge(xp, hact_p, a_p, wbd_p, sb_p, wa_bf, wb_bf, wo_bf, tm=MERGE_ROWS)
    ffn_rows = min(FFN_ROWS, seq)
    y_p, conv_p = _ffn_prompt(x1_p, g_ffn, w_up_bf, cw, cb, w_down_bf, g_fin, tm=ffn_rows, rows_per_seq=seq)

    a_s = _attn_sample(hact_s, cache_k[0].reshape(n_streams, n_cache * N_HEADS, HEAD_DIM),
                       cache_v[0].reshape(n_streams, n_cache * N_HEADS, HEAD_DIM),
                       _sample_bias(rel_bias[0], n_cache, n_new), n_streams=n_streams, n_new=n_new)
    wbd_s, sb_s = _sgu_block_weights(w_s[0], b_s[0], n_new)
    x1_s = _merge(xs, hact_s, a_s, wbd_s, sb_s, wa_bf, wb_bf, wo_bf, tm=ms)
    y_s, conv_g, conv_v = _ffn_sample(x1_s, g_ffn, w_up_bf, cw, cb, w_down_bf, g_fin,
                                      cache_ffn_conv[0], seq=n_new)

    heads = lambda t, b, s: t.reshape(1, b, s, N_HEADS, HEAD_DIM)
    new_conv_prompt = conv_p[None]
    new_conv_sample = jnp.concatenate([conv_g, conv_v], axis=-1)[None]
    return (
        y_p.reshape(batch, seq, D_MODEL),
        y_s.reshape(n_streams, n_new, D_MODEL),
        heads(k_tail, batch, keep),
        heads(v_tail, batch, keep),
        heads(k_new, n_streams, n_new),
        heads(v_new, n_streams, n_new),
        vb_new.reshape(1, n_streams, n_new, D_SGU),
        new_conv_prompt,
        new_conv_sample,
    )
```

```python
import functools

import numpy as np
import jax
import jax.numpy as jnp
from jax import lax
from jax.experimental import pallas as pl
from jax.experimental.pallas import tpu as pltpu

D_MODEL = 2048
CHUNK = 64
N_LEFT_CHUNKS = 8
KV_WINDOW = N_LEFT_CHUNKS * CHUNK
D_ATTN = D_MODEL // 2
N_HEADS = 8
HEAD_DIM = D_ATTN // N_HEADS
MAX_REL = 256
D_SGU = D_MODEL // 2
N_GROUPS = 8
GROUP_DIM = D_SGU // N_GROUPS
SGU_CHUNK = 128
D_FF = 5632
CONV_W = 3
EPS = 1e-6
PAST_LEN = 2048
D_IN = 3 * D_ATTN + 2 * D_SGU + 2 * D_MODEL
NEG_INF = -1e30

COL = 1024
COL_Q, COL_K, COL_V, COL_U, COL_VB, COL_GA, COL_GB = 0, 1, 2, 3, 4, 5, 7
N_COL_BLOCKS = D_IN // COL

PROJ_ROWS = 1024
MERGE_ROWS = 512
FFN_ROWS = 1024
FFN_SUB = 256
PROJ_COL_CHUNK = 512
PROJ_ROW_CHUNK = 256
ATTN_QB = 256
LOG2E = float(np.log2(np.e))
SGU_SUB = 256
FF_TILE = 512
N_FF_TILES = D_FF // FF_TILE
CARRY_ROWS = 8
BF16_SUBLANES = 16

VMEM_LIMIT = 56 * 1024 * 1024
VMEM_LIMIT_FFN = 60 * 1024 * 1024

BF16 = jnp.bfloat16
F32 = jnp.float32


def _rms(x, g):
    inv = lax.rsqrt(jnp.mean(x * x, axis=-1, keepdims=True) + EPS)
    return (x * inv) * g


def _gelu(x):
    return 0.5 * x * (1.0 + lax.erf(x * (2.0 ** -0.5)))


def _sigmoid(x):
    return 0.5 * jnp.tanh(0.5 * x) + 0.5


def _dot(a, b):
    return jnp.dot(a, b, preferred_element_type=F32)


def _dot_nt(a, b):
    return lax.dot_general(a, b, (((1,), (1,)), ((), ())), preferred_element_type=F32)


def _in_proj_kernel(x_ref, g_ref, w_ref, sg_ref, *refs, tail, vb_tail, n_cast):
    cast_in, refs = refs[:n_cast], refs[n_cast:]
    h_ref, kt_ref, vt_ref = refs[:3]
    vbt_ref = refs[3] if vb_tail else None
    cast_out, refs = refs[3 + vb_tail:3 + vb_tail + n_cast], refs[3 + vb_tail + n_cast:]
    wbf_ref = refs[0] if len(refs) == 2 else None
    xn_ref = refs[-1]
    j = pl.program_id(1)
    tm = x_ref.shape[0]
    head = tm - tail

    @pl.when(j == 0)
    def _():
        xn_ref[...] = _rms(x_ref[...], g_ref[...]).astype(BF16)

    def side_cast():
        for src, dst in zip(cast_in, cast_out):
            dst[...] = src[...].astype(BF16)

    def weights():
        if wbf_ref is None:
            return w_ref
        wbf_ref[...] = w_ref[...].astype(BF16)
        return wbf_ref

    def by_cols(act, tail_ref=None):
        w_ref = weights()
        for c in range(COL // PROJ_COL_CHUNK):
            cols = slice(c * PROJ_COL_CHUNK, (c + 1) * PROJ_COL_CHUNK)
            acc = _dot(xn_ref[...], w_ref[:, cols])
            h_ref[:, cols] = act(acc).astype(BF16)
            if tail_ref is not None:
                tail_ref[:, cols] = acc[head:, :]
        side_cast()

    @pl.when(j == COL_Q)
    def _():
        by_cols(lambda a: a)

    @pl.when(j == COL_K)
    def _():
        by_cols(lambda a: a, kt_ref)

    @pl.when(j == COL_V)
    def _():
        by_cols(lambda a: a, vt_ref)

    @pl.when(j == COL_U)
    def _():
        by_cols(_gelu)

    @pl.when(j == COL_VB)
    def _():
        w_bf = weights()
        for r in range(tm // PROJ_ROW_CHUNK):
            lo = r * PROJ_ROW_CHUNK
            rows = slice(lo, lo + PROJ_ROW_CHUNK)
            vb = _rms(_gelu(_dot(xn_ref[rows, :], w_bf[...])), sg_ref[...])
            h_ref[rows, :] = vb.astype(BF16)
            if vbt_ref is not None and lo >= head:
                vbt_ref[lo - head:lo - head + PROJ_ROW_CHUNK, :] = vb
        side_cast()

    @pl.when(j >= COL_GA)
    def _():
        by_cols(_sigmoid)


def _in_proj(x, g, w, sg, *, tm, rows_per_seq, tail, vb_tail, cast=()):
    m = x.shape[0]
    emit_w = w.dtype != BF16
    assert not emit_w or m == tm, "the weight copy is written once, so it needs a single row block"
    blocks_per_seq = rows_per_seq // tm
    n_seq = m // rows_per_seq
    assert tail <= tm and (tm - tail) % PROJ_ROW_CHUNK == 0
    n_tails = 3 if vb_tail else 2
    n_steps = (m // tm) * N_COL_BLOCKS
    tail_spec = pl.BlockSpec((tail, COL), lambda i, j: (i // blocks_per_seq, 0))
    tail_shape = jax.ShapeDtypeStruct((n_seq * tail, COL), F32)

    def slab_spec(w):
        rows = BF16_SUBLANES
        while w.shape[0] // rows > n_steps:
            rows *= 2
        assert w.shape[0] % rows == 0
        last = w.shape[0] // rows - 1
        return pl.BlockSpec((rows, w.shape[1]), lambda i, j: (jnp.minimum(i * N_COL_BLOCKS + j, last), 0))

    slab_specs = [slab_spec(c) for c in cast]
    w_spec = pl.BlockSpec((D_MODEL, COL), lambda i, j: (0, j))
    return pl.pallas_call(
        functools.partial(_in_proj_kernel, tail=tail, vb_tail=vb_tail, n_cast=len(cast)),
        grid=(m // tm, N_COL_BLOCKS),
        in_specs=[
            pl.BlockSpec((tm, D_MODEL), lambda i, j: (i, 0)),
            pl.BlockSpec((1, D_MODEL), lambda i, j: (0, 0)),
            w_spec,
            pl.BlockSpec((1, D_SGU), lambda i, j: (0, 0)),
        ] + slab_specs,
        out_specs=[pl.BlockSpec((tm, COL), lambda i, j: (i, j))] + [tail_spec] * n_tails + slab_specs
        + [w_spec] * emit_w,
        out_shape=[jax.ShapeDtypeStruct((m, D_IN), BF16)] + [tail_shape] * n_tails
        + [jax.ShapeDtypeStruct(c.shape, BF16) for c in cast] + [jax.ShapeDtypeStruct(w.shape, BF16)] * emit_w,
        scratch_shapes=[pltpu.VMEM((tm, D_MODEL), BF16)],
        compiler_params=pltpu.CompilerParams(
            dimension_semantics=("arbitrary", "arbitrary"), vmem_limit_bytes=VMEM_LIMIT),
        name="in_proj",
    )(x, g, w, sg, *cast)


def _attn_prompt_kernel(q_ref, k0_ref, k1_ref, k2_ref, v0_ref, v1_ref, v2_ref, diag_ref, o_ref, bias_ref):
    qb = pl.program_id(1)
    k_refs = (k0_ref, k1_ref, k2_ref)
    v_refs = (v0_ref, v1_ref, v2_ref)
    scale2 = HEAD_DIM ** -0.5 * LOG2E
    n_keys = 3 * ATTN_QB

    @pl.when((pl.program_id(0) == 0) & (qb == 0))
    def _():
        qi = lax.broadcasted_iota(jnp.int32, (ATTN_QB, n_keys), 0) // CHUNK
        ck = lax.broadcasted_iota(jnp.int32, (ATTN_QB, n_keys), 1) // CHUNK - (2 * ATTN_QB) // CHUNK
        allowed = (ck <= qi) & (qi - ck <= N_LEFT_CHUNKS)
        width = diag_ref.shape[1]
        for h in range(N_HEADS):
            row = jnp.broadcast_to(diag_ref[h:h + 1, :], (ATTN_QB, width))
            skew = pltpu.roll(row, width - (ATTN_QB - 1), 1, stride=1, stride_axis=0)
            bias_ref[h] = jnp.where(allowed, skew[:, :n_keys] * LOG2E, NEG_INF)

    ones = jnp.ones((3 * ATTN_QB, HEAD_DIM), BF16)

    def attend(mask_missing):
        for h in range(N_HEADS):
            cols = slice(h * HEAD_DIM, (h + 1) * HEAD_DIM)
            k = jnp.concatenate([r[:, cols] for r in k_refs], axis=0)
            v = jnp.concatenate([jnp.concatenate([r[:, cols] for r in v_refs], axis=0), ones], axis=1)
            s = _dot_nt(q_ref[:, cols], k) * scale2 + bias_ref[h]
            if mask_missing:
                s = jnp.concatenate(
                    [s[:, r * ATTN_QB:(r + 1) * ATTN_QB] + jnp.where(qb - 2 + r >= 0, 0.0, NEG_INF).astype(F32)
                     for r in range(3)], axis=1)
            e = jnp.exp2(s - s.max(-1, keepdims=True)).astype(BF16)
            o = _dot(e, v)
            o_ref[:, cols] = (o[:, :HEAD_DIM] / o[:, HEAD_DIM:]).astype(BF16)

    @pl.when(qb < KV_WINDOW // ATTN_QB)
    def _():
        attend(True)

    @pl.when(qb >= KV_WINDOW // ATTN_QB)
    def _():
        attend(False)


def _rel_bias_diagonals(rel_bias, n, m, d0):
    dist = np.arange(n + m - 1) - (m - 1) + d0
    diag = rel_bias[:, np.clip(dist, -MAX_REL, MAX_REL) + MAX_REL].astype(F32)
    rev = diag[:, ::-1]
    return jnp.concatenate([rev, rev[:, :1]], axis=1)


def _rel_bias_table(rel_bias, n, m, d0):
    length = n + m - 1
    padded = _rel_bias_diagonals(rel_bias, n, m, d0)
    skew = jnp.tile(padded, (1, n))[:, :n * length].reshape(-1, n, length)
    return skew[:, :, n - 1:n - 1 + m]


def _attn_prompt(hact, rel_bias, *, batch, seq):
    nqb = seq // ATTN_QB
    n_keys = 3 * ATTN_QB
    diag = _rel_bias_diagonals(rel_bias, ATTN_QB, n_keys, 2 * ATTN_QB)
    kv_spec = lambda col, r: pl.BlockSpec(
        (ATTN_QB, COL), lambda b, t: (b * nqb + jnp.maximum(t - 2 + r, 0), col))
    return pl.pallas_call(
        _attn_prompt_kernel,
        grid=(batch, nqb),
        in_specs=[pl.BlockSpec((ATTN_QB, COL), lambda b, t: (b * nqb + t, COL_Q))]
        + [kv_spec(COL_K, r) for r in range(3)] + [kv_spec(COL_V, r) for r in range(3)]
        + [pl.BlockSpec(diag.shape, lambda b, t: (0, 0))],
        out_specs=pl.BlockSpec((ATTN_QB, D_ATTN), lambda b, t: (b * nqb + t, 0)),
        out_shape=jax.ShapeDtypeStruct((batch * seq, D_ATTN), BF16),
        scratch_shapes=[pltpu.VMEM((N_HEADS, ATTN_QB, n_keys), F32)],
        compiler_params=pltpu.CompilerParams(
            dimension_semantics=("arbitrary", "arbitrary"), vmem_limit_bytes=VMEM_LIMIT),
        name="attn_prompt",
    )(hact, hact, hact, hact, hact, hact, hact, diag)


def _attn_sample_kernel(q_ref, kn_ref, vn_ref, kc_ref, vc_ref, bias_ref, o_ref):
    n_cache = kc_ref.shape[1] // N_HEADS
    scale = HEAD_DIM ** -0.5
    for h in range(N_HEADS):
        cols = slice(h * HEAD_DIM, (h + 1) * HEAD_DIM)
        head_rows = pl.ds(h, n_cache, stride=N_HEADS)
        q = q_ref[:, cols]
        s_c = _dot_nt(q, kc_ref[0, head_rows, :].astype(BF16)) * scale + bias_ref[h, :, :n_cache]
        s_n = _dot_nt(q, kn_ref[:, cols]) * scale + bias_ref[h, :, n_cache:]
        mx = jnp.maximum(s_c.max(-1, keepdims=True), s_n.max(-1, keepdims=True))
        e_c, e_n = jnp.exp(s_c - mx), jnp.exp(s_n - mx)
        den = e_c.sum(-1, keepdims=True) + e_n.sum(-1, keepdims=True)
        o = (_dot(e_c.astype(BF16), vc_ref[0, head_rows, :].astype(BF16))
             + _dot(e_n.astype(BF16), vn_ref[:, cols]))
        o_ref[:, cols] = (o / den).astype(BF16)


def _sample_bias(rel_bias, n_cache, n_new):
    q_pos = PAST_LEN + np.arange(n_new)
    k_pos = np.concatenate([PAST_LEN - n_cache + np.arange(n_cache), PAST_LEN + np.arange(n_new)])
    cq, ck = q_pos[:, None] // CHUNK, k_pos[None, :] // CHUNK
    allowed = (ck <= cq) & (cq - ck <= N_LEFT_CHUNKS)
    bias = jnp.concatenate([_rel_bias_table(rel_bias, n_new, n_cache, n_cache),
                            _rel_bias_table(rel_bias, n_new, n_new, 0)], axis=2)
    return jnp.where(allowed[None], bias, NEG_INF)


def _attn_sample(hact, cache_k, cache_v, bias, *, n_streams, n_new):
    n_cache = cache_k.shape[1] // N_HEADS
    new_spec = lambda col: pl.BlockSpec((n_new, COL), lambda b: (b, col))
    cache_spec = pl.BlockSpec((1, n_cache * N_HEADS, HEAD_DIM), lambda b: (b, 0, 0))
    return pl.pallas_call(
        _attn_sample_kernel,
        grid=(n_streams,),
        in_specs=[new_spec(COL_Q), new_spec(COL_K), new_spec(COL_V), cache_spec, cache_spec,
                  pl.BlockSpec((N_HEADS, n_new, n_cache + n_new), lambda b: (0, 0, 0))],
        out_specs=pl.BlockSpec((n_new, D_ATTN), lambda b: (b, 0)),
        out_shape=jax.ShapeDtypeStruct((n_streams * n_new, D_ATTN), BF16),
        compiler_params=pltpu.CompilerParams(
            dimension_semantics=("arbitrary",), vmem_limit_bytes=VMEM_LIMIT),
        name="attn_sample",
    )(hact, hact, hact, cache_k, cache_v, bias)


def _merge_kernel(x_ref, u_ref, vb_ref, ga0_ref, ga1_ref, gb0_ref, gb1_ref, a_ref,
                  wbd_ref, sb_ref, wa_ref, wb_ref, wo_ref, o_ref, s_ref):
    tm = x_ref.shape[0]
    for c in range(tm // SGU_SUB):
        rows = slice(c * SGU_SUB, (c + 1) * SGU_SUB)
        for g in range(N_GROUPS):
            cols = slice(g * GROUP_DIM, (g + 1) * GROUP_DIM)
            mixed = _dot(wbd_ref[g], vb_ref[rows, cols]) + sb_ref[:, g:g + 1]
            s_ref[rows, cols] = (u_ref[rows, cols].astype(F32) * mixed).astype(BF16)
    pa = _dot(a_ref[...], wa_ref[...])
    pb = _dot(s_ref[...], wb_ref[...])
    half = D_MODEL // 2
    m0 = ga0_ref[...].astype(F32) * pa[:, :half] + gb0_ref[...].astype(F32) * pb[:, :half]
    m1 = ga1_ref[...].astype(F32) * pa[:, half:] + gb1_ref[...].astype(F32) * pb[:, half:]
    m = jnp.concatenate([m0, m1], axis=-1).astype(BF16)
    o_ref[...] = x_ref[...] + _dot(m, wo_ref[...])


def _sgu_block_weights(w_s, b_s, chunk):
    reps = SGU_SUB // chunk
    pos = np.arange(SGU_SUB)
    place = jnp.asarray(pos[:, None] % chunk == np.arange(chunk)[None, :], BF16)
    same_block = pos[:, None] // chunk == pos[None, :] // chunk
    w = (w_s * np.tril(np.ones((SGU_CHUNK, SGU_CHUNK), np.float32)))[:, :chunk, :chunk].astype(BF16)
    tiled = jnp.einsum('rc,gcd,sd->grs', place, w, place, preferred_element_type=F32)
    wbd = jnp.where(same_block[None], tiled, 0.0).astype(BF16)
    bias = jnp.tile(b_s[:, :chunk].T, (reps, 1)).astype(F32)
    return wbd, bias


def _merge(x, hact, a, wbd, sbias, wa, wb, wo, *, tm):
    m = x.shape[0]
    hcol = lambda col: pl.BlockSpec((tm, COL), lambda i: (i, col))
    const = lambda shape: pl.BlockSpec(shape, lambda i: (0,) * len(shape), pipeline_mode=pl.Buffered(1))
    return pl.pallas_call(
        _merge_kernel,
        grid=(m // tm,),
        in_specs=[pl.BlockSpec((tm, D_MODEL), lambda i: (i, 0)),
                  hcol(COL_U), hcol(COL_VB), hcol(COL_GA), hcol(COL_GA + 1), hcol(COL_GB), hcol(COL_GB + 1),
                  pl.BlockSpec((tm, D_ATTN), lambda i: (i, 0)),
                  const((N_GROUPS, SGU_SUB, SGU_SUB)), const((SGU_SUB, N_GROUPS)),
                  const((D_ATTN, D_MODEL)), const((D_SGU, D_MODEL)), const((D_MODEL, D_MODEL))],
        out_specs=pl.BlockSpec((tm, D_MODEL), lambda i: (i, 0)),
        out_shape=jax.ShapeDtypeStruct((m, D_MODEL), F32),
        scratch_shapes=[pltpu.VMEM((tm, D_SGU), BF16)],
        compiler_params=pltpu.CompilerParams(
            dimension_semantics=("arbitrary",), vmem_limit_bytes=VMEM_LIMIT),
        name="merge",
    )(x, hact, hact, hact, hact, hact, hact, a, wbd, sbias, wa, wb, wo)


def _conv_gate(hg, hv, prev_g, prev_v, cwg, cwv, cbg, cbv):
    def conv(h, prev, cw, cb):
        return cb + cw[0:1, :] * prev(2) + cw[1:2, :] * prev(1) + cw[2:3, :] * h
    return (_gelu(conv(hg, prev_g, cwg, cbg)) * conv(hv, prev_v, cwv, cbv)).astype(BF16)


def _ffn_prologue(j, x_ref, g_ref, y_ref, xn_ref):
    @pl.when(j == 0)
    def _():
        x = x_ref[...]
        xn_ref[...] = _rms(x, g_ref[...]).astype(BF16)
        y_ref[...] = x


def _ffn_epilogue(j, gf_ref, y_ref):
    @pl.when(j == N_FF_TILES - 1)
    def _():
        y_ref[...] = _rms(y_ref[...], gf_ref[...])


def _ffn_prompt_kernel(x_ref, g_ref, wg_ref, wv_ref, cw_ref, cb_ref, wd_ref, gf_ref,
                       y_ref, hl_ref, xn_ref, cg_ref, cv_ref, *, blocks_per_seq):
    i = pl.program_id(0)
    j = pl.program_id(1)
    tm = x_ref.shape[0]

    @pl.when(i % blocks_per_seq == 0)
    def _():
        cg_ref[j] = jnp.zeros(cg_ref.shape[1:], F32)
        cv_ref[j] = jnp.zeros(cv_ref.shape[1:], F32)

    def delayed(h, carry):
        def prev(k):
            head = jnp.concatenate([carry, h[:CARRY_ROWS]], axis=0)[CARRY_ROWS - k:2 * CARRY_ROWS - k]
            return jnp.concatenate([head, pltpu.roll(h, k, axis=0)[CARRY_ROWS:]], axis=0)
        return prev

    def step(first, last):
        if first:
            xn_ref[...] = _rms(x_ref[...], g_ref[...]).astype(BF16)
        up = lambda w_ref: jnp.concatenate(
            [_dot(xn_ref[r:r + FFN_SUB, :], w_ref[...]) for r in range(0, tm, FFN_SUB)], axis=0)
        hg = up(wg_ref)
        hv = up(wv_ref)
        act = _conv_gate(hg, hv, delayed(hg, cg_ref[j]), delayed(hv, cv_ref[j]),
                         cw_ref[j], cw_ref[N_FF_TILES + j], cb_ref[j], cb_ref[N_FF_TILES + j])
        y = (x_ref if first else y_ref)[...] + _dot(act, wd_ref[...])
        y_ref[...] = _rms(y, gf_ref[...]) if last else y
        last_g, last_v = hg[tm - CARRY_ROWS:], hv[tm - CARRY_ROWS:]
        cg_ref[j] = last_g
        cv_ref[j] = last_v
        hl_ref[0, 0, j] = last_g
        hl_ref[0, 1, j] = last_v

    @pl.when(j == 0)
    def _():
        step(True, False)

    @pl.when((j > 0) & (j < N_FF_TILES - 1))
    def _():
        step(False, False)

    @pl.when(j == N_FF_TILES - 1)
    def _():
        step(False, True)


def _ffn_sample_kernel(x_ref, g_ref, wg_ref, wv_ref, cwg_ref, cwv_ref, cbg_ref, cbv_ref, wd_ref, gf_ref,
                       pg_ref, pv_ref, y_ref, hg_ref, hv_ref, xn_ref, *, seq):
    j = pl.program_id(1)
    tm = x_ref.shape[0]
    _ffn_prologue(j, x_ref, g_ref, y_ref, xn_ref)
    up = lambda w_ref: jnp.concatenate(
        [_dot(xn_ref[r:r + FFN_SUB, :], w_ref[...]) for r in range(0, tm, FFN_SUB)], axis=0)
    hg = up(wg_ref)
    hv = up(wv_ref)
    keep = lambda h: h.reshape(tm // seq, seq, FF_TILE)[:, seq - (CONV_W - 1):, :]
    hg_ref[...] = keep(hg)
    hv_ref[...] = keep(hv)
    pos = lax.broadcasted_iota(jnp.int32, (tm, FF_TILE), 0) % seq

    def delayed(h, hist_ref):
        def expand(t):
            n = hist_ref.shape[0]
            return jnp.broadcast_to(hist_ref[:, t:t + 1, :], (n, seq, FF_TILE)).reshape(tm, FF_TILE)

        def prev(k):
            rolled = pltpu.roll(h, k, axis=0)
            if k == 1:
                return jnp.where(pos == 0, expand(1), rolled)
            return jnp.where(pos == 0, expand(0), jnp.where(pos == 1, expand(1), rolled))
        return prev

    act = _conv_gate(hg, hv, delayed(hg, pg_ref), delayed(hv, pv_ref),
                     cwg_ref[...], cwv_ref[...], cbg_ref[...], cbv_ref[...])
    y_ref[...] += _dot(act, wd_ref[...])
    _ffn_epilogue(j, gf_ref, y_ref)


def _ffn_common_specs(tm):
    row = lambda i, j: (i, 0)
    fixed = lambda i, j: (0, 0)
    gate = lambda i, j: (0, j)
    val = lambda i, j: (0, N_FF_TILES + j)
    return [
        pl.BlockSpec((tm, D_MODEL), row),
        pl.BlockSpec((1, D_MODEL), fixed),
        pl.BlockSpec((D_MODEL, FF_TILE), gate),
        pl.BlockSpec((D_MODEL, FF_TILE), val),
        pl.BlockSpec((CONV_W, FF_TILE), gate),
        pl.BlockSpec((CONV_W, FF_TILE), val),
        pl.BlockSpec((1, FF_TILE), gate),
        pl.BlockSpec((1, FF_TILE), val),
        pl.BlockSpec((FF_TILE, D_MODEL), lambda i, j: (j, 0)),
        pl.BlockSpec((1, D_MODEL), fixed),
    ]


def _ffn_prompt(x, g, w_up, conv_w, conv_b, w_down, gf, *, tm, rows_per_seq):
    m = x.shape[0]
    blocks_per_seq = rows_per_seq // tm
    n_tiles = 2 * N_FF_TILES
    cw_tiles = conv_w.reshape(CONV_W, n_tiles, FF_TILE).transpose(1, 0, 2)
    cb_tiles = conv_b.reshape(n_tiles, 1, FF_TILE)
    row = lambda i, j: (i, 0)
    fixed2 = lambda i, j: (0, 0)
    fixed3 = lambda i, j: (0, 0, 0)
    last_shape = (1, 2, N_FF_TILES, CARRY_ROWS, FF_TILE)
    carry = pltpu.VMEM((N_FF_TILES, CARRY_ROWS, FF_TILE), F32)
    y, h_last = pl.pallas_call(
        functools.partial(_ffn_prompt_kernel, blocks_per_seq=blocks_per_seq),
        grid=(m // tm, N_FF_TILES),
        in_specs=[
            pl.BlockSpec((tm, D_MODEL), row),
            pl.BlockSpec((1, D_MODEL), fixed2),
            pl.BlockSpec((D_MODEL, FF_TILE), lambda i, j: (0, j)),
            pl.BlockSpec((D_MODEL, FF_TILE), lambda i, j: (0, N_FF_TILES + j)),
            pl.BlockSpec((n_tiles, CONV_W, FF_TILE), fixed3),
            pl.BlockSpec((n_tiles, 1, FF_TILE), fixed3),
            pl.BlockSpec((FF_TILE, D_MODEL), lambda i, j: (j, 0)),
            pl.BlockSpec((1, D_MODEL), fixed2),
        ],
        out_specs=[pl.BlockSpec((tm, D_MODEL), row), pl.BlockSpec(last_shape, lambda i, j: (i, 0, 0, 0, 0))],
        out_shape=[jax.ShapeDtypeStruct((m, D_MODEL), F32),
                   jax.ShapeDtypeStruct((m // tm,) + last_shape[1:], F32)],
        scratch_shapes=[pltpu.VMEM((tm, D_MODEL), BF16), carry, carry],
        compiler_params=pltpu.CompilerParams(
            dimension_semantics=("arbitrary", "arbitrary"), vmem_limit_bytes=VMEM_LIMIT_FFN),
        name="ffn_prompt",
    )(x, g, w_up, w_up, cw_tiles, cb_tiles, w_down, gf)
    tail = h_last[blocks_per_seq - 1::blocks_per_seq, :, :, CARRY_ROWS - (CONV_W - 1):, :]
    return y, tail.transpose(0, 3, 1, 2, 4).reshape(tail.shape[0], CONV_W - 1, 2 * D_FF)


def _ffn_sample(x, g, w_up, conv_w, conv_b, w_down, gf, hist, *, seq):
    m = x.shape[0]
    n_streams = m // seq
    hist_g = pl.BlockSpec((n_streams, CONV_W - 1, FF_TILE), lambda i, j: (0, 0, j))
    hist_v = pl.BlockSpec((n_streams, CONV_W - 1, FF_TILE), lambda i, j: (0, 0, N_FF_TILES + j))
    h_spec = pl.BlockSpec((n_streams, CONV_W - 1, FF_TILE), lambda i, j: (0, 0, j))
    h_shape = jax.ShapeDtypeStruct((n_streams, CONV_W - 1, D_FF), F32)
    return pl.pallas_call(
        functools.partial(_ffn_sample_kernel, seq=seq),
        grid=(1, N_FF_TILES),
        in_specs=_ffn_common_specs(m) + [hist_g, hist_v],
        out_specs=[pl.BlockSpec((m, D_MODEL), lambda i, j: (0, 0)), h_spec, h_spec],
        out_shape=[jax.ShapeDtypeStruct((m, D_MODEL), F32), h_shape, h_shape],
        scratch_shapes=[pltpu.VMEM((m, D_MODEL), BF16)],
        compiler_params=pltpu.CompilerParams(
            dimension_semantics=("arbitrary", "arbitrary"), vmem_limit_bytes=VMEM_LIMIT),
        name="ffn_sample",
    )(x, g, w_up, w_up, conv_w, conv_w, conv_b, conv_b, w_down, gf, hist, hist)


def kernel(x_prompt, x_sample, cache_k, cache_v, cache_ffn_conv, norm_mix_g, w_in, rel_bias, sgu_norm_g, w_s, b_s,
           w_branch_a, w_branch_b, w_out, norm_ffn_g, w_up, conv_w, conv_b, w_down, norm_final_g):
    depth = w_in.shape[0]
    assert depth == 1, "single-layer trunk"
    batch, seq, _ = x_prompt.shape
    n_streams, n_new, _ = x_sample.shape
    n_cache = cache_k.shape[2]
    keep = min(KV_WINDOW, seq)

    row = lambda v: v.reshape(1, -1).astype(F32)
    g_mix, g_sgu, g_ffn, g_fin = row(norm_mix_g[0]), row(sgu_norm_g[0]), row(norm_ffn_g[0]), row(norm_final_g)
    cw, cb = conv_w[0].astype(F32), row(conv_b[0])

    ms = n_streams * n_new
    xs = x_sample.reshape(ms, D_MODEL)
    hact_s, k_new, v_new, vb_new, w_in_bf = _in_proj(xs, g_mix, w_in[0], g_sgu, tm=ms, rows_per_seq=ms, tail=ms,
                                                     vb_tail=True)

    xp = x_prompt.reshape(batch * seq, D_MODEL)
    hact_p, k_tail, v_tail, wa_bf, wb_bf, wo_bf, w_up_bf, w_down_bf = _in_proj(
        xp, g_mix, w_in_bf, g_sgu, tm=min(PROJ_ROWS, seq), rows_per_seq=seq, tail=keep, vb_tail=False,
        cast=(w_branch_a[0], w_branch_b[0], w_out[0], w_up[0], w_down[0]))
    a_p = _attn_prompt(hact_p, rel_bias[0], batch=batch, seq=seq)
    wbd_p, sb_p = _sgu_block_weights(w_s[0], b_s[0], SGU_CHUNK)
    x1_p = _merge(xp, hact_p, a_p, wbd_p, sb_p, wa_bf, wb_bf, wo_bf, tm=MERGE_ROWS)
    ffn_rows = min(FFN_ROWS, seq)
    y_p, conv_p = _ffn_prompt(x1_p, g_ffn, w_up_bf, cw, cb, w_down_bf, g_fin, tm=ffn_rows, rows_per_seq=seq)

    a_s = _attn_sample(hact_s, cache_k[0].reshape(n_streams, n_cache * N_HEADS, HEAD_DIM),
                       cache_v[0].reshape(n_streams, n_cache * N_HEADS, HEAD_DIM),
                       _sample_bias(rel_bias[0], n_cache, n_new), n_streams=n_streams, n_new=n_new)
    wbd_s, sb_s = _sgu_block_weights(w_s[0], b_s[0], n_new)
    x1_s = _merge(xs, hact_s, a_s, wbd_s, sb_s, wa_bf, wb_bf, wo_bf, tm=ms)
    y_s, conv_g, conv_v = _ffn_sample(x1_s, g_ffn, w_up_bf, cw, cb, w_down_bf, g_fin,
                                      cache_ffn_conv[0], seq=n_new)

    heads = lambda t, b, s: t.reshape(1, b, s, N_HEADS, HEAD_DIM)
    new_conv_prompt = conv_p[None]
    new_conv_sample = jnp.concatenate([conv_g, conv_v], axis=-1)[None]
    return (
        y_p.reshape(batch, seq, D_MODEL),
        y_s.reshape(n_streams, n_new, D_MODEL),
        heads(k_tail, batch, keep),
        heads(v_tail, batch, keep),
        heads(k_new, n_streams, n_new),
        heads(v_new, n_streams, n_new),
        vb_new.reshape(1, n_streams, n_new, D_SGU),
        new_conv_prompt,
        new_conv_sample,
    )
```
